```python
import math
import jax, jax.numpy as jnp
from jax import lax
import numpy as np

D_MODEL = 1024
BATCH = 8
SEQ = 4096
DEPTH = 2

ATTN_HEADS = 4
ATTN_QK_DIM = 64
ATTN_V_DIM = 2 * ATTN_QK_DIM
Q_BLOCK = 128
DN_HEADS = 4
DN_HEAD_DIM = 128
CONV_WIDTH = 5
CHUNK = 64
N_EXPERTS = 16
EXPERT_HIDDEN = 1024
CAPACITY_FACTOR = 2
N_BRANCHES = 2
DEEPNORM_ALPHA = (2 * DEPTH) ** 0.25
DEEPNORM_BETA = (8 * DEPTH) ** -0.25
LN_EPS = 1e-5
RMS_EPS = 1e-6

ATTN_QK_WIDTH = ATTN_HEADS * 2 * ATTN_QK_DIM
ATTN_V_WIDTH = ATTN_HEADS * ATTN_V_DIM
DN_WIDTH = DN_HEADS * DN_HEAD_DIM
SPLIT_SIZES = (ATTN_QK_WIDTH, ATTN_QK_WIDTH, ATTN_V_WIDTH, 3 * DN_WIDTH, DN_WIDTH,
               2 * DN_HEADS, 2 * DN_HEADS, N_BRANCHES * D_MODEL)
IN_WIDTH = sum(SPLIT_SIZES)

kernel_name = "hybrid_diffattn_gdn_ecmoe_deepnorm"


def layer_norm(x, g, b):
    xf = x.astype(jnp.float32)
    mu = jnp.mean(xf, axis=-1, keepdims=True)
    var = jnp.mean(jnp.square(xf - mu), axis=-1, keepdims=True)
    return ((xf - mu) * lax.rsqrt(var + LN_EPS)).astype(x.dtype) * g + b


def rms_norm(x, g):
    xf = x.astype(jnp.float32)
    r = lax.rsqrt(jnp.mean(jnp.square(xf), axis=-1, keepdims=True) + RMS_EPS)
    return (xf * r).astype(x.dtype) * g


def l2_normalize(x):
    xf = x.astype(jnp.float32)
    return (xf * lax.rsqrt(jnp.sum(jnp.square(xf), axis=-1, keepdims=True) + RMS_EPS)).astype(x.dtype)


def split_columns(p):
    outs, start = [], 0
    for size in SPLIT_SIZES:
        outs.append(p[..., start:start + size])
        start += size
    return outs


def alibi_slopes(n_heads):
    return 2.0 ** (-8.0 * jnp.arange(1, n_heads + 1, dtype=jnp.float32) / n_heads)


def diff_attention(q, k, v, lam):
    b, h, _, s, dh = q.shape
    nb = s // Q_BLOCK
    scale = dh ** -0.5
    slopes = alibi_slopes(h)[:, None, None, None]
    kpos = jnp.arange(s)
    qb = q.reshape(b, h, 2, nb, Q_BLOCK, dh).transpose(3, 0, 1, 2, 4, 5)
    starts = jnp.arange(nb) * Q_BLOCK

    def block(args):
        qi, st = args
        sc = jnp.einsum('bhmqd,bhmkd->bhmqk', qi, k).astype(jnp.float32) * scale
        dist = jnp.abs((st + jnp.arange(Q_BLOCK))[:, None] - kpos[None, :]).astype(jnp.float32)
        sc = sc - slopes * dist
        p = jax.nn.softmax(sc, axis=-1)
        a = p[:, :, 0] - lam * p[:, :, 1]
        return jnp.einsum('bhqk,bhkd->bhqd', a.astype(v.dtype), v)

    o = lax.map(block, (qb, starts))
    return o.transpose(1, 2, 0, 3, 4).reshape(b, h, s, v.shape[-1])


def gated_delta_rule(q, k, v, beta, g):
    out_dtype = v.dtype
    q, k, v, beta, g = (t.astype(jnp.float32) for t in (q, k, v, beta, g))
    b, h, s, dk = q.shape
    dv = v.shape[-1]
    n = s // CHUNK
    q = q.reshape(b, h, n, CHUNK, dk)
    k = k.reshape(b, h, n, CHUNK, dk)
    v = v.reshape(b, h, n, CHUNK, dv)
    beta = beta.reshape(b, h, n, CHUNK)
    gc = jnp.cumsum(g.reshape(b, h, n, CHUNK), axis=-1)
    incl = jnp.tril(jnp.ones((CHUNK, CHUNK), dtype=bool))
    strict = jnp.tril(jnp.ones((CHUNK, CHUNK), dtype=bool), -1)
    gdiff = gc[..., :, None] - gc[..., None, :]
    decay = jnp.where(incl, jnp.exp(jnp.where(incl, gdiff, 0.0)), 0.0)
    kb = k * beta[..., None]
    lmat = jnp.where(strict, jnp.einsum('bhnid,bhnjd->bhnij', kb, k) * decay, 0.0)
    rhs = jnp.concatenate([v * beta[..., None], kb * jnp.exp(gc)[..., None]], axis=-1)
    sol = lax.linalg.triangular_solve(lmat + jnp.eye(CHUNK, dtype=jnp.float32), rhs,
                                      left_side=True, lower=True, unit_diagonal=True)
    u, w = sol[..., :dv], sol[..., dv:]
    attn = jnp.einsum('bhnid,bhnjd->bhnij', q, k) * decay
    q_dec = q * jnp.exp(gc)[..., None]
    k_dec = k * jnp.exp(gc[..., -1:] - gc)[..., None]
    cdecay = jnp.exp(gc[..., -1])
    xs = tuple(jnp.moveaxis(t, 2, 0) for t in (u, w, q_dec, k_dec, attn, cdecay))

    def step(state, inp):
        u_c, w_c, q_c, k_c, a_c, d_c = inp
        v_new = u_c - jnp.einsum('bhck,bhkv->bhcv', w_c, state)
        o_c = jnp.einsum('bhck,bhkv->bhcv', q_c, state) + jnp.einsum('bhcj,bhjv->bhcv', a_c, v_new)
        state = state * d_c[..., None, None] + jnp.einsum('bhck,bhcv->bhkv', k_c, v_new)
        return state, o_c

    _, o = lax.scan(step, jnp.zeros((b, h, dk, dv), jnp.float32), xs)
    return jnp.moveaxis(o, 0, 2).reshape(b, h, s, dv).astype(out_dtype)


def short_conv(u, w):
    kw, c = w.shape
    return lax.conv_general_dilated(u, w[:, None, :], window_strides=(1,),
                                    padding=[(kw // 2, kw // 2)],
                                    dimension_numbers=('NWC', 'WIO', 'NWC'),
                                    feature_group_count=c)


def token_mixer(x, w_in, lam_q1, lam_k1, lam_q2, lam_k2, attn_norm_g, conv_w, a_log,
                dt_bias, dn_norm_g, w_pa, w_pd, w_o, layer_idx):
    b, s, _ = x.shape
    proj = jnp.einsum('bsd,de->bse', x, w_in)
    aq, ak, av, dqkv, dgate, dbeta, dalpha, bgates = split_columns(proj)

    qa = aq.reshape(b, s, ATTN_HEADS, 2, ATTN_QK_DIM).transpose(0, 2, 3, 1, 4)
    ka = ak.reshape(b, s, ATTN_HEADS, 2, ATTN_QK_DIM).transpose(0, 2, 3, 1, 4)
    va = av.reshape(b, s, ATTN_HEADS, ATTN_V_DIM).transpose(0, 2, 1, 3)
    lambda_init = 0.8 - 0.6 * math.exp(-0.3 * layer_idx)
    lam = (jnp.exp(jnp.sum(lam_q1.astype(jnp.float32) * lam_k1.astype(jnp.float32)))
           - jnp.exp(jnp.sum(lam_q2.astype(jnp.float32) * lam_k2.astype(jnp.float32)))
           + lambda_init)
    oa = diff_attention(qa, ka, va, lam)
    oa = rms_norm(oa, attn_norm_g) * (1.0 - lambda_init)
    oa = oa.transpose(0, 2, 1, 3).reshape(b, s, ATTN_V_WIDTH)

    qkv = jax.nn.silu(short_conv(dqkv, conv_w))
    qd, kd, vd = (t.reshape(b, s, DN_HEADS, DN_HEAD_DIM).transpose(0, 2, 1, 3)
                  for t in (qkv[..., :DN_WIDTH], qkv[..., DN_WIDTH:2 * DN_WIDTH], qkv[..., 2 * DN_WIDTH:]))
    qd = l2_normalize(qd) * (DN_HEAD_DIM ** -0.5)
    kd = l2_normalize(kd)
    beta = jax.nn.sigmoid(dbeta.reshape(b, s, 2, DN_HEADS).transpose(2, 0, 3, 1))
    araw = dalpha.reshape(b, s, 2, DN_HEADS).transpose(2, 0, 3, 1)
    g = -jnp.exp(a_log)[:, None, :, None] * jax.nn.softplus(araw + dt_bias[:, None, :, None])
    o_f = gated_delta_rule(qd, kd, vd, beta[0], g[0])
    flip = lambda t: jnp.flip(t, axis=2)
    o_b = flip(gated_delta_rule(flip(qd), flip(kd), flip(vd), flip(beta[1]), flip(g[1])))
    od = (o_f + o_b).transpose(0, 2, 1, 3)
    od = rms_norm(od, dn_norm_g) * jax.nn.silu(dgate.reshape(b, s, DN_HEADS, DN_HEAD_DIM))
    od = od.reshape(b, s, DN_WIDTH)

    ga = jax.nn.sigmoid(bgates[..., :D_MODEL])
    gd = jax.nn.sigmoid(bgates[..., D_MODEL:])
    merged = ga * jnp.einsum('bsi,id->bsd', oa, w_pa) + gd * jnp.einsum('bsi,id->bsd', od, w_pd)
    return jnp.einsum('bsd,de->bse', merged, w_o)


def expert_choice_ffn(x, w_router, w_gate, w_up, w_down):
    b, s, _ = x.shape
    cap = CAPACITY_FACTOR * s // N_EXPERTS
    aff = jax.nn.softmax(jnp.einsum('bsd,de->bse', x, w_router).astype(jnp.float32), axis=-1)
    gate, idx = lax.top_k(aff.transpose(0, 2, 1), cap)
    bidx = jnp.arange(b)[:, None, None]
    xg = x[bidx, idx]
    hid = jax.nn.silu(jnp.einsum('becd,edf->becf', xg, w_gate)) * jnp.einsum('becd,edf->becf', xg, w_up)
    yo = jnp.einsum('becf,efd->becd', hid, w_down) * gate[..., None].astype(x.dtype)
    return jnp.zeros_like(x).at[bidx, idx].add(yo)


def setup_inputs(seed: int = 0) -> dict:
    key = jax.random.key(seed)
    ks = jax.random.split(key, 24)
    f32 = jnp.float32
    nrm = lambda k, shape, std: jax.random.normal(k, shape, f32) * std
    col_scale = np.concatenate([
        np.ones(2 * ATTN_QK_WIDTH), np.full(ATTN_V_WIDTH, DEEPNORM_BETA),
        np.ones(2 * DN_WIDTH), np.full(DN_WIDTH, DEEPNORM_BETA),
        np.ones(DN_WIDTH + 4 * DN_HEADS + N_BRANCHES * D_MODEL)]).astype(np.float32)
    dt = jnp.exp(jax.random.uniform(ks[10], (DEPTH, 2, DN_HEADS), f32, math.log(1e-3), math.log(1e-1)))
    return {
        "x": jax.random.normal(ks[0], (BATCH, SEQ, D_MODEL), f32),
        "ln0_g": 1.0 + nrm(ks[1], (D_MODEL,), 0.02),
        "ln0_b": nrm(ks[2], (D_MODEL,), 0.02),
        "w_in": nrm(ks[3], (DEPTH, D_MODEL, IN_WIDTH), D_MODEL ** -0.5) * jnp.asarray(col_scale),
        "lam_q1": nrm(ks[4], (DEPTH, ATTN_QK_DIM), 0.1),
        "lam_k1": nrm(ks[5], (DEPTH, ATTN_QK_DIM), 0.1),
        "lam_q2": nrm(ks[6], (DEPTH, ATTN_QK_DIM), 0.1),
        "lam_k2": nrm(ks[7], (DEPTH, ATTN_QK_DIM), 0.1),
        "attn_norm_g": 1.0 + nrm(ks[8], (DEPTH, ATTN_V_DIM), 0.02),
        "conv_w": nrm(ks[9], (DEPTH, CONV_WIDTH, 3 * DN_WIDTH), CONV_WIDTH ** -0.5),
        "a_log": jnp.log(jax.random.uniform(ks[11], (DEPTH, 2, DN_HEADS), f32, 1.0, 16.0)),
        "dt_bias": dt + jnp.log(-jnp.expm1(-dt)),
        "dn_norm_g": 1.0 + nrm(ks[12], (DEPTH, DN_HEAD_DIM), 0.02),
        "w_pa": nrm(ks[13], (DEPTH, ATTN_V_WIDTH, D_MODEL), ATTN_V_WIDTH ** -0.5 * DEEPNORM_BETA),
        "w_pd": nrm(ks[14], (DEPTH, DN_WIDTH, D_MODEL), DN_WIDTH ** -0.5 * DEEPNORM_BETA),
        "w_o": nrm(ks[15], (DEPTH, D_MODEL, D_MODEL), D_MODEL ** -0.5 * DEEPNORM_BETA),
        "ln1_g": 1.0 + nrm(ks[16], (DEPTH, D_MODEL), 0.02),
        "ln1_b": nrm(ks[17], (DEPTH, D_MODEL), 0.02),
        "w_router": nrm(ks[18], (DEPTH, D_MODEL, N_EXPERTS), D_MODEL ** -0.5),
        "w_gate": nrm(ks[19], (DEPTH, N_EXPERTS, D_MODEL, EXPERT_HIDDEN), D_MODEL ** -0.5),
        "w_up": nrm(ks[20], (DEPTH, N_EXPERTS, D_MODEL, EXPERT_HIDDEN), D_MODEL ** -0.5),
        "w_down": nrm(ks[21], (DEPTH, N_EXPERTS, EXPERT_HIDDEN, D_MODEL), EXPERT_HIDDEN ** -0.5 * DEEPNORM_BETA),
        "ln2_g": 1.0 + nrm(ks[22], (DEPTH, D_MODEL), 0.02),
        "ln2_b": nrm(ks[23], (DEPTH, D_MODEL), 0.02),
    }


def reference(x, ln0_g, ln0_b, w_in, lam_q1, lam_k1, lam_q2, lam_k2, attn_norm_g, conv_w,
              a_log, dt_bias, dn_norm_g, w_pa, w_pd, w_o, ln1_g, ln1_b, w_router, w_gate,
              w_up, w_down, ln2_g, ln2_b):
    h = layer_norm(x, ln0_g, ln0_b)
    for l in range(DEPTH):
        mix = token_mixer(h, w_in[l], lam_q1[l], lam_k1[l], lam_q2[l], lam_k2[l], attn_norm_g[l],
                          conv_w[l], a_log[l], dt_bias[l], dn_norm_g[l], w_pa[l], w_pd[l], w_o[l], l)
        h = layer_norm(DEEPNORM_ALPHA * h + mix, ln1_g[l], ln1_b[l])
        ffn = expert_choice_ffn(h, w_router[l], w_gate[l], w_up[l], w_down[l])
        h = layer_norm(DEEPNORM_ALPHA * h + ffn, ln2_g[l], ln2_b[l])
    return h
```

```python
import functools
import math

import jax
import jax.numpy as jnp
from jax import lax
from jax.experimental import pallas as pl
from jax.experimental.pallas import tpu as pltpu

F32 = jnp.float32
BF16 = jnp.bfloat16
HIGHEST = lax.Precision.HIGHEST

D_MODEL = 1024
DEPTH = 2
ATTN_HEADS = 4
ATTN_QK_DIM = 64
ATTN_V_DIM = 128
DN_HEADS = 4
DN_HEAD_DIM = 128
CONV_WIDTH = 5
CHUNK = 64
N_EXPERTS = 16
EXPERT_HIDDEN = 1024
CAPACITY_FACTOR = 2
DEEPNORM_ALPHA = (2 * DEPTH) ** 0.25
LN_EPS = 1e-5
RMS_EPS = 1e-6

LANES = 128
GATES_W = 2 * D_MODEL
MAIN_W = GATES_W + 3 * 512 + 3 * 512 + 512
COL_AQ = GATES_W // LANES
COL_AK = COL_AQ + 4
COL_AV = COL_AK + 4
COL_DQ = COL_AV + 4
COL_DK = COL_DQ + 4
COL_DV = COL_DK + 4
COL_DG = COL_DV + 4

VMEM_LIMIT = 56 * 1024 * 1024


def _cp(sem):
    return pltpu.CompilerParams(dimension_semantics=sem, vmem_limit_bytes=VMEM_LIMIT)


def _sigmoid(x):
    return 1.0 / (1.0 + jnp.exp(-x))


def _silu(x):
    return x * _sigmoid(x)


def _layer_norm(y, g, b):
    mu = jnp.mean(y, axis=-1, keepdims=True)
    yc = y - mu
    var = jnp.mean(yc * yc, axis=-1, keepdims=True)
    return yc * lax.rsqrt(var + LN_EPS) * g + b


def _ln_kernel(x_ref, g_ref, b_ref, o_ref):
    o_ref[...] = _layer_norm(x_ref[...], g_ref[...], b_ref[...])


def _add_ln_kernel(h_ref, f_ref, g_ref, b_ref, o_ref):
    o_ref[...] = _layer_norm(DEEPNORM_ALPHA * h_ref[...] + f_ref[...], g_ref[...], b_ref[...])


def _ln(x2, g, b, tm):
    t, d = x2.shape
    row = pl.BlockSpec((tm, d), lambda i: (i, 0))
    vec = pl.BlockSpec((1, d), lambda i: (0, 0))
    return pl.pallas_call(
        _ln_kernel, grid=(t // tm,), in_specs=[row, vec, vec], out_specs=row,
        out_shape=jax.ShapeDtypeStruct((t, d), F32), compiler_params=_cp(("parallel",)),
        name="ln0")(x2, g.reshape(1, d), b.reshape(1, d))


def _add_ln(h2, f2, g, b, tm):
    t, d = h2.shape
    row = pl.BlockSpec((tm, d), lambda i: (i, 0))
    vec = pl.BlockSpec((1, d), lambda i: (0, 0))
    return pl.pallas_call(
        _add_ln_kernel, grid=(t // tm,), in_specs=[row, row, vec, vec], out_specs=row,
        out_shape=jax.ShapeDtypeStruct((t, d), F32), compiler_params=_cp(("parallel",)),
        name="add_ln")(h2, f2, g.reshape(1, d), b.reshape(1, d))


def _inproj_kernel(h_ref, wm_ref, ws_ref, alog_ref, dtb_ref, mf_ref, mb_ref, om_ref, os_ref, *, ncol):
    h = h_ref[...]
    hb = h.astype(BF16)
    for c in range(0, MAIN_W, ncol):
        om_ref[:, c:c + ncol] = jnp.dot(hb, wm_ref[:, c:c + ncol],
                                        preferred_element_type=F32).astype(BF16)
    sm = jnp.dot(h, ws_ref[...], preferred_element_type=F32, precision=HIGHEST)
    lane = lax.broadcasted_iota(jnp.int32, sm.shape, 1)
    beta = _sigmoid(sm)
    z = sm + dtb_ref[...]
    softplus = jnp.maximum(z, 0.0) + jnp.log(1.0 + jnp.exp(-jnp.abs(z)))
    g = jnp.where((lane >= 8) & (lane < 16), -jnp.exp(alog_ref[...]) * softplus, 0.0)
    gf = jnp.dot(mf_ref[...], g, preferred_element_type=F32, precision=HIGHEST)
    gb = jnp.dot(mb_ref[...], g, preferred_element_type=F32, precision=HIGHEST)
    os_ref[...] = jnp.where(lane < 8, beta, jnp.where(lane < 12, gf, gb))


def _inproj(h2, wm, ws, alog, dtb, mf, mb, tm):
    t, d = h2.shape
    const = lambda shape: pl.BlockSpec(shape, lambda i: (0, 0))
    return pl.pallas_call(
        functools.partial(_inproj_kernel, ncol=512),
        grid=(t // tm,),
        in_specs=[pl.BlockSpec((tm, d), lambda i: (i, 0)), const((d, MAIN_W)), const((d, LANES)),
                  const((1, LANES)), const((1, LANES)), const((tm, tm)), const((tm, tm))],
        out_specs=[pl.BlockSpec((tm, MAIN_W), lambda i: (i, 0)),
                   pl.BlockSpec((tm, LANES), lambda i: (i, 0))],
        out_shape=[jax.ShapeDtypeStruct((t, MAIN_W), BF16), jax.ShapeDtypeStruct((t, LANES), F32)],
        compiler_params=_cp(("parallel",)), name="inproj")(h2, wm, ws, alog, dtb, mf, mb)


def _attn_kernel(slope_ref, lamp_ref, g_ref, q_ref, k_ref, v_ref, o_ref, *, tq, tk, seq, lambda_init):
    hh = pl.program_id(1)
    qi = pl.program_id(2)
    slope = slope_ref[hh]
    lp = lamp_ref[...]
    lam = (jnp.exp(jnp.sum(lp[0:1] * lp[1:2], axis=-1, keepdims=True))
           - jnp.exp(jnp.sum(lp[2:3] * lp[3:4], axis=-1, keepdims=True)) + lambda_init)
    q = q_ref[...] * (ATTN_QK_DIM ** -0.5)
    lane = lax.broadcasted_iota(jnp.int32, q.shape, 1)
    zero = jnp.zeros_like(q)
    q1 = jnp.where(lane < ATTN_QK_DIM, q, zero)
    q2 = jnp.where(lane >= ATTN_QK_DIM, q, zero)
    rel = (lax.broadcasted_iota(jnp.int32, (tq, tk), 0)
           - lax.broadcasted_iota(jnp.int32, (tq, tk), 1)).astype(F32)
    nt = (((1,), (1,)), ((), ()))

    def body(j, carry):
        m1, l1, a1, m2, l2, a2 = carry
        k = k_ref[pl.ds(j * tk, tk), :]
        v = v_ref[pl.ds(j * tk, tk), :]
        off = (qi * tq - j * tk).astype(F32)
        bias = slope * jnp.abs(rel + off)

        def one(qm, m, l, a):
            s = lax.dot_general(qm, k, nt, preferred_element_type=F32) - bias
            m_new = jnp.maximum(m, jnp.max(s, axis=-1, keepdims=True))
            alpha = jnp.exp(m - m_new)
            p = jnp.exp(s - m_new)
            l_new = alpha * l + jnp.sum(p, axis=-1, keepdims=True)
            a_new = alpha * a + jnp.dot(p.astype(BF16), v, preferred_element_type=F32)
            return m_new, l_new, a_new

        m1, l1, a1 = one(q1, m1, l1, a1)
        m2, l2, a2 = one(q2, m2, l2, a2)
        return m1, l1, a1, m2, l2, a2

    neg = jnp.full((tq, 1), -1e30, F32)
    zl = jnp.zeros((tq, 1), F32)
    za = jnp.zeros((tq, ATTN_V_DIM), F32)
    m1, l1, a1, m2, l2, a2 = lax.fori_loop(0, seq // tk, body, (neg, zl, za, neg, zl, za))
    o = a1 / l1 - lam * (a2 / l2)
    r = lax.rsqrt(jnp.mean(o * o, axis=-1, keepdims=True) + RMS_EPS)
    o_ref[...] = (o * r * g_ref[...] * (1.0 - lambda_init)).astype(o_ref.dtype)


def _attention(pm, slopes, lamp, norm_g, lambda_init, tq, tk):
    b, s, _ = pm.shape
    kern = functools.partial(_attn_kernel, tq=tq, tk=tk, seq=s, lambda_init=lambda_init)
    return pl.pallas_call(
        kern, grid=(b, ATTN_HEADS, s // tq),
        in_specs=[pl.BlockSpec(memory_space=pltpu.SMEM),
                  pl.BlockSpec((4, ATTN_QK_DIM), lambda bi, h, i: (0, 0)),
                  pl.BlockSpec((1, ATTN_V_DIM), lambda bi, h, i: (0, 0)),
                  pl.BlockSpec((None, tq, LANES), lambda bi, h, i: (bi, i, COL_AQ + h)),
                  pl.BlockSpec((None, s, LANES), lambda bi, h, i: (bi, 0, COL_AK + h)),
                  pl.BlockSpec((None, s, LANES), lambda bi, h, i: (bi, 0, COL_AV + h))],
        out_specs=pl.BlockSpec((None, tq, LANES), lambda bi, h, i: (bi, i, h)),
        out_shape=jax.ShapeDtypeStruct((b, s, ATTN_HEADS * ATTN_V_DIM), BF16),
        compiler_params=_cp(("parallel", "parallel", "parallel")), name="diff_attn",
    )(slopes, lamp, norm_g.reshape(1, ATTN_V_DIM), pm, pm, pm)


def _lane_pick(x, idx):
    lane = lax.broadcasted_iota(jnp.int32, x.shape, 1)
    return jnp.sum(jnp.where(lane == idx, x, 0.0), axis=1, keepdims=True)


def _row_pick(x, idx):
    row = lax.broadcasted_iota(jnp.int32, x.shape, 0)
    return jnp.sum(jnp.where(row == idx, x, 0.0), axis=0, keepdims=True)


def _unit_tri_inverse(lm):
    n = lm.shape[0]
    eye = (lax.broadcasted_iota(jnp.int32, (n, n), 0)
           == lax.broadcasted_iota(jnp.int32, (n, n), 1)).astype(F32)
    p = -lm
    inv = eye + p
    steps = int(math.log2(n)) - 1
    for _ in range(steps):
        p = jnp.dot(p, p, preferred_element_type=F32, precision=HIGHEST)
        inv = inv + jnp.dot(inv, p, preferred_element_type=F32, precision=HIGHEST)
    return inv


def _gdn_kernel(qr_ref, kr_ref, vr_ref, gt_ref, sm_ref, cw_ref, ng_ref, o_ref,
                q_s, k_s, v_s, u_s, w_s, qd_s, kdt_s, at_s, dec_s, of_s, ob_s, *, seq):
    hh = pl.program_id(1)
    nchunk = seq // CHUNK
    row = lax.broadcasted_iota(jnp.int32, (seq, DN_HEAD_DIM), 0)

    def conv_silu(ref, widx):
        u = ref[...].astype(F32)
        w = cw_ref[widx]
        acc = jnp.zeros_like(u)
        for j in range(CONV_WIDTH):
            d = j - CONV_WIDTH // 2
            if d == 0:
                sh = u
            else:
                sh = pltpu.roll(u, (-d) % seq, axis=0)
                ok = (row + d >= 0) & (row + d < seq)
                sh = jnp.where(ok, sh, 0.0)
            acc = acc + sh * w[j:j + 1, :]
        return _silu(acc)

    def l2n(x):
        return x * lax.rsqrt(jnp.sum(x * x, axis=-1, keepdims=True) + RMS_EPS)

    q_s[...] = l2n(conv_silu(qr_ref, 0)) * (DN_HEAD_DIM ** -0.5)
    k_s[...] = l2n(conv_silu(kr_ref, 1))
    v_s[...] = conv_silu(vr_ref, 2)

    ci = lax.broadcasted_iota(jnp.int32, (CHUNK, CHUNK), 0)
    cj = lax.broadcasted_iota(jnp.int32, (CHUNK, CHUNK), 1)
    nt = (((1,), (1,)), ((), ()))

    def local(c, _):
        rows = pl.ds(pl.multiple_of(c * CHUNK, CHUNK), CHUNK)
        qc = q_s[rows, :]
        kc = k_s[rows, :]
        vc = v_s[rows, :]
        smc = sm_ref[rows, :]
        smt = smc.T
        kb16 = kc.astype(BF16)
        kk = lax.dot_general(kb16, kb16, nt, preferred_element_type=F32)
        qk = lax.dot_general(qc.astype(BF16), kb16, nt, preferred_element_type=F32)
        for d in range(2):
            beta = _lane_pick(smc, d * DN_HEADS + hh)
            gcol = _lane_pick(smc, 8 + d * DN_HEADS + hh)
            grow = _row_pick(smt, 8 + d * DN_HEADS + hh)
            incl = (cj <= ci) if d == 0 else (cj >= ci)
            strict = (cj < ci) if d == 0 else (cj > ci)
            decay = jnp.where(incl, jnp.exp(jnp.where(incl, gcol - grow, 0.0)), 0.0)
            lm = jnp.where(strict, beta * kk * decay, 0.0)
            inv = _unit_tri_inverse(lm)
            eg = jnp.exp(gcol)
            rhs = jnp.concatenate([vc * beta, kc * (beta * eg)], axis=1)
            sol = jnp.dot(inv, rhs, preferred_element_type=F32, precision=HIGHEST)
            glast = gcol[CHUNK - 1:CHUNK, :] if d == 0 else gcol[0:1, :]
            u_s[d, rows, :] = sol[:, :DN_HEAD_DIM]
            w_s[d, rows, :] = sol[:, DN_HEAD_DIM:].astype(BF16)
            qd_s[d, rows, :] = (qc * eg).astype(BF16)
            kdt_s[d, pl.ds(pl.multiple_of(c * DN_HEAD_DIM, DN_HEAD_DIM), DN_HEAD_DIM), :] = (
                kc * jnp.exp(glast - gcol)).T.astype(BF16)
            at_s[d, rows, :] = (qk * decay).astype(BF16)
            dec_s[d, pl.ds(c, 1), :] = jnp.broadcast_to(jnp.exp(glast), (1, DN_HEAD_DIM))
        return 0

    lax.fori_loop(0, nchunk, local, 0)

    def scan(n, states):
        new_states = []
        for d in range(2):
            c = n if d == 0 else nchunk - 1 - n
            st = states[d]
            rows = pl.ds(pl.multiple_of(c * CHUNK, CHUNK), CHUNK)
            s16 = st.astype(BF16)
            ws = jnp.dot(w_s[d, rows, :], s16, preferred_element_type=F32)
            qs = jnp.dot(qd_s[d, rows, :], s16, preferred_element_type=F32)
            v_new = u_s[d, rows, :] - ws
            vn16 = v_new.astype(BF16)
            o_c = qs + jnp.dot(at_s[d, rows, :], vn16, preferred_element_type=F32)
            kdt = kdt_s[d, pl.ds(pl.multiple_of(c * DN_HEAD_DIM, DN_HEAD_DIM), DN_HEAD_DIM), :]
            st = st * dec_s[d, pl.ds(c, 1), :] + jnp.dot(kdt, vn16, preferred_element_type=F32)
            if d == 0:
                of_s[rows, :] = o_c
            else:
                ob_s[rows, :] = o_c
            new_states.append(st)
        return tuple(new_states)

    z = jnp.zeros((DN_HEAD_DIM, DN_HEAD_DIM), F32)
    lax.fori_loop(0, nchunk, scan, (z, z))

    o = of_s[...] + ob_s[...]
    r = lax.rsqrt(jnp.mean(o * o, axis=-1, keepdims=True) + RMS_EPS)
    o_ref[...] = (o * r * ng_ref[...] * _silu(gt_ref[...].astype(F32))).astype(o_ref.dtype)


def _gdn(pm, sm, cw, norm_g):
    b, s, _ = pm.shape
    nchunk = s // CHUNK
    col = lambda c0: pl.BlockSpec((None, s, LANES), lambda bi, h: (bi, 0, c0 + h))
    hd = DN_HEAD_DIM
    return pl.pallas_call(
        functools.partial(_gdn_kernel, seq=s), grid=(b, DN_HEADS),
        in_specs=[col(COL_DQ), col(COL_DK), col(COL_DV), col(COL_DG),
                  pl.BlockSpec((None, s, LANES), lambda bi, h: (bi, 0, 0)),
                  pl.BlockSpec((3, None, CONV_WIDTH, hd), lambda bi, h: (0, h, 0, 0)),
                  pl.BlockSpec((1, hd), lambda bi, h: (0, 0))],
        out_specs=pl.BlockSpec((None, s, LANES), lambda bi, h: (bi, 0, h)),
        out_shape=jax.ShapeDtypeStruct((b, s, DN_HEADS * hd), BF16),
        scratch_shapes=[pltpu.VMEM((s, hd), F32), pltpu.VMEM((s, hd), F32), pltpu.VMEM((s, hd), F32),
                        pltpu.VMEM((2, s, hd), F32), pltpu.VMEM((2, s, hd), BF16),
                        pltpu.VMEM((2, s, hd), BF16), pltpu.VMEM((2, nchunk * hd, CHUNK), BF16),
                        pltpu.VMEM((2, s, CHUNK), BF16), pltpu.VMEM((2, nchunk, hd), F32),
                        pltpu.VMEM((s, hd), F32), pltpu.VMEM((s, hd), F32)],
        compiler_params=_cp(("parallel", "parallel")), name="gdn",
    )(pm, pm, pm, pm, sm, cw, norm_g.reshape(1, hd))


def _outproj_kernel(h_ref, oa_ref, od_ref, ga_ref, gd_ref, wpa_ref, wpd_ref, wo_ref, g_ref, b_ref,
                    wr_ref, h1_ref, aff_ref):
    pa = jnp.dot(oa_ref[...], wpa_ref[...], preferred_element_type=F32)
    pd = jnp.dot(od_ref[...], wpd_ref[...], preferred_element_type=F32)
    merged = _sigmoid(ga_ref[...].astype(F32)) * pa + _sigmoid(gd_ref[...].astype(F32)) * pd
    mix = jnp.dot(merged.astype(BF16), wo_ref[...], preferred_element_type=F32)
    h1 = _layer_norm(DEEPNORM_ALPHA * h_ref[...] + mix, g_ref[...], b_ref[...])
    h1_ref[...] = h1
    logits = jnp.dot(h1, wr_ref[...], preferred_element_type=F32, precision=HIGHEST)
    lane = lax.broadcasted_iota(jnp.int32, logits.shape, 1)
    logits = jnp.where(lane < N_EXPERTS, logits, -1e30)
    e = jnp.exp(logits - jnp.max(logits, axis=-1, keepdims=True))
    aff_ref[...] = e / jnp.sum(e, axis=-1, keepdims=True)


def _outproj(h2, oa2, od2, pm2, wpa, wpd, wo, g, b, wr, tm):
    t, d = h2.shape
    row = lambda w, c=0: pl.BlockSpec((tm, w), lambda i: (i, c))
    const = lambda shape: pl.BlockSpec(shape, lambda i: (0, 0))
    return pl.pallas_call(
        _outproj_kernel, grid=(t // tm,),
        in_specs=[row(d), row(512), row(512), row(d, 0), row(d, 1),
                  const((512, d)), const((512, d)), const((d, d)), const((1, d)), const((1, d)),
                  const((d, LANES))],
        out_specs=[row(d), row(LANES)],
        out_shape=[jax.ShapeDtypeStruct((t, d), F32), jax.ShapeDtypeStruct((t, LANES), F32)],
        compiler_params=_cp(("parallel",)), name="outproj",
    )(h2, oa2, od2, pm2, pm2, wpa, wpd, wo, g.reshape(1, d), b.reshape(1, d), wr)


def _topk_kernel(aff_ref, tri_ref, idx_ref, pin_s, *, seq, cap, blk):
    aff = aff_ref[...]
    bits = pltpu.bitcast(aff, jnp.int32)
    ones = lambda cond: jnp.where(cond, 1.0, 0.0)

    def refine(i, thr):
        cand = thr | (jnp.int32(1) << (30 - i))
        cnt = jnp.sum(ones(bits >= cand), axis=0, keepdims=True)
        return jnp.where(cnt >= cap, cand, thr)

    thr = lax.fori_loop(0, 31, refine, jnp.zeros((1, LANES), jnp.int32))
    gt = ones(bits > thr)
    eq = ones(bits == thr)
    need = cap - jnp.sum(gt, axis=0, keepdims=True)

    def prefix(mask):
        carry = jnp.zeros((1, LANES), F32)
        for r in range(0, seq, blk):
            m = mask[r:r + blk, :]
            loc = jnp.dot(tri_ref[...], m.astype(BF16), preferred_element_type=F32)
            pin_s[r:r + blk, :] = loc + carry
            carry = carry + jnp.sum(m, axis=0, keepdims=True)
        return pin_s[...]

    eq_rank = prefix(eq)
    sel = jnp.maximum(gt, eq * ones(eq_rank <= need))
    prefix(sel)
    slot = lax.broadcasted_iota(jnp.int32, (blk, cap), 1).astype(F32)
    for e in range(N_EXPERTS):
        acc = jnp.zeros((1, cap), F32)
        for r in range(0, seq, blk):
            col = pin_s[r:r + blk, e:e + 1]
            acc = acc + jnp.sum(ones(col <= slot), axis=0, keepdims=True)
        idx_ref[e:e + 1, :] = acc.astype(jnp.int32)


def _topk(aff3, tri, cap, blk):
    b, s, _ = aff3.shape
    return pl.pallas_call(
        functools.partial(_topk_kernel, seq=s, cap=cap, blk=blk), grid=(b,),
        in_specs=[pl.BlockSpec((None, s, LANES), lambda bi: (bi, 0, 0)),
                  pl.BlockSpec((blk, blk), lambda bi: (0, 0))],
        out_specs=pl.BlockSpec((None, N_EXPERTS, cap), lambda bi: (bi, 0, 0)),
        out_shape=jax.ShapeDtypeStruct((b, N_EXPERTS, cap), jnp.int32),
        scratch_shapes=[pltpu.VMEM((s, LANES), F32)],
        compiler_params=_cp(("parallel",)), name="topk")(aff3, tri)


def _gather_kernel(idx_ref, x_ref, aff_ref, xg_ref, gg_ref, xs_s, *, cap):
    def body(s, _):
        t = idx_ref[0, s]
        xs_s[pl.ds(s, 1), :] = x_ref[pl.ds(t, 1), :]
        gg_ref[pl.ds(s, 1), :] = aff_ref[pl.ds(t, 1), :]
        return 0

    lax.fori_loop(0, cap, body, 0, unroll=8)
    xg_ref[...] = xs_s[...].astype(xg_ref.dtype)


def _gather(idx3, h3, aff3, cap):
    b, s, d = h3.shape
    return pl.pallas_call(
        functools.partial(_gather_kernel, cap=cap), grid=(b, N_EXPERTS),
        in_specs=[pl.BlockSpec((None, 1, cap), lambda bi, e: (bi * N_EXPERTS + e, 0, 0),
                               memory_space=pltpu.SMEM),
                  pl.BlockSpec((None, s, d), lambda bi, e: (bi, 0, 0)),
                  pl.BlockSpec((None, s, LANES), lambda bi, e: (bi, 0, 0))],
        out_specs=[pl.BlockSpec((None, None, cap, d), lambda bi, e: (bi, e, 0, 0)),
                   pl.BlockSpec((None, None, cap, LANES), lambda bi, e: (bi, e, 0, 0))],
        out_shape=[jax.ShapeDtypeStruct((b, N_EXPERTS, cap, d), BF16),
                   jax.ShapeDtypeStruct((b, N_EXPERTS, cap, LANES), F32)],
        scratch_shapes=[pltpu.VMEM((cap, d), F32)],
        compiler_params=_cp(("parallel", "arbitrary")), name="moe_gather")(idx3, h3, aff3)


def _ffn_kernel(xg_ref, gg_ref, wg_ref, wu_ref, wd_ref, yo_ref):
    e = pl.program_id(0)
    x = xg_ref[...]
    a = jnp.dot(x, wg_ref[...], preferred_element_type=F32)
    u = jnp.dot(x, wu_ref[...], preferred_element_type=F32)
    hid = (_silu(a) * u).astype(BF16)
    y = jnp.dot(hid, wd_ref[...], preferred_element_type=F32)
    yo_ref[...] = y * _lane_pick(gg_ref[...], e)


def _ffn(xg, gg, wg, wu, wd):
    b, ne, cap, d = xg.shape
    f = wg.shape[-1]
    return pl.pallas_call(
        _ffn_kernel, grid=(ne, b),
        in_specs=[pl.BlockSpec((None, None, cap, d), lambda e, bi: (bi, e, 0, 0)),
                  pl.BlockSpec((None, None, cap, LANES), lambda e, bi: (bi, e, 0, 0)),
                  pl.BlockSpec((None, d, f), lambda e, bi: (e, 0, 0)),
                  pl.BlockSpec((None, d, f), lambda e, bi: (e, 0, 0)),
                  pl.BlockSpec((None, f, d), lambda e, bi: (e, 0, 0))],
        out_specs=pl.BlockSpec((None, None, cap, d), lambda e, bi: (bi, e, 0, 0)),
        out_shape=jax.ShapeDtypeStruct((b, ne, cap, d), F32),
        compiler_params=_cp(("parallel", "parallel")), name="moe_ffn")(xg, gg, wg, wu, wd)


def _combine_kernel(idx_ref, yo_ref, acc_ref, *, cap):
    @pl.when(pl.program_id(1) == 0)
    def _():
        acc_ref[...] = jnp.zeros_like(acc_ref)

    def body(s, _):
        t = idx_ref[0, s]
        acc_ref[pl.ds(t, 1), :] = acc_ref[pl.ds(t, 1), :] + yo_ref[pl.ds(s, 1), :]
        return 0

    lax.fori_loop(0, cap, body, 0, unroll=8)


def _combine(idx3, yo, s):
    b, ne, cap, d = yo.shape
    return pl.pallas_call(
        functools.partial(_combine_kernel, cap=cap), grid=(b, ne),
        in_specs=[pl.BlockSpec((None, 1, cap), lambda bi, e: (bi * N_EXPERTS + e, 0, 0),
                               memory_space=pltpu.SMEM),
                  pl.BlockSpec((None, None, cap, d), lambda bi, e: (bi, e, 0, 0))],
        out_specs=pl.BlockSpec((None, s, d), lambda bi, e: (bi, 0, 0)),
        out_shape=jax.ShapeDtypeStruct((b, s, d), F32),
        compiler_params=_cp(("parallel", "arbitrary")), name="moe_combine")(idx3, yo)


def _block_tri(n, lower):
    i = jnp.arange(n)[:, None]
    j = jnp.arange(n)[None, :]
    same = (i // CHUNK) == (j // CHUNK)
    return (same & ((j <= i) if lower else (j >= i))).astype(F32)


def kernel(x, ln0_g, ln0_b, w_in, lam_q1, lam_k1, lam_q2, lam_k2, attn_norm_g, conv_w, a_log, dt_bias,
           dn_norm_g, w_pa, w_pd, w_o, ln1_g, ln1_b, w_router, w_gate, w_up, w_down, ln2_g, ln2_b):
    b, s, d = x.shape
    t = b * s
    cap = CAPACITY_FACTOR * s // N_EXPERTS
    tm = min(512, s)
    blk = min(512, s)
    tq = min(256, s)
    tk = min(512, s)

    slopes = 2.0 ** (-8.0 * jnp.arange(1, ATTN_HEADS + 1, dtype=F32) / ATTN_HEADS)
    mf = _block_tri(tm, True)
    mb = _block_tri(tm, False)
    tri = jnp.tril(jnp.ones((blk, blk), BF16))
    pad16 = lambda v: jnp.pad(v.reshape(1, -1), ((0, 0), (8, LANES - 16)))

    h = _ln(x.reshape(t, d), ln0_g, ln0_b, tm)
    for l in range(DEPTH):
        lambda_init = 0.8 - 0.6 * math.exp(-0.3 * l)
        w = w_in[l]
        wm = jnp.concatenate([w[:, 3600:], w[:, :3584]], axis=1).astype(BF16)
        ws = jnp.pad(w[:, 3584:3600], ((0, 0), (0, LANES - 16)))
        pm, sm = _inproj(h, wm, ws, pad16(a_log[l]), pad16(dt_bias[l]), mf, mb, tm)
        pm3 = pm.reshape(b, s, MAIN_W)
        lamp = jnp.stack([lam_q1[l], lam_k1[l], lam_q2[l], lam_k2[l]])
        oa = _attention(pm3, slopes, lamp, attn_norm_g[l], lambda_init, tq, tk)
        cw = conv_w[l].reshape(CONV_WIDTH, 3, DN_HEADS, DN_HEAD_DIM).transpose(1, 2, 0, 3)
        od = _gdn(pm3, sm.reshape(b, s, LANES), cw, dn_norm_g[l])
        wr = jnp.pad(w_router[l], ((0, 0), (0, LANES - N_EXPERTS)))
        h1, aff = _outproj(h, oa.reshape(t, -1), od.reshape(t, -1), pm,
                           w_pa[l].astype(BF16), w_pd[l].astype(BF16), w_o[l].astype(BF16),
                           ln1_g[l], ln1_b[l], wr, tm)
        aff3 = aff.reshape(b, s, LANES)
        idx = _topk(aff3, tri, cap, blk)
        idx3 = idx.reshape(b * N_EXPERTS, 1, cap)
        xg, gg = _gather(idx3, h1.reshape(b, s, d), aff3, cap)
        yo = _ffn(xg, gg, w_gate[l].astype(BF16), w_up[l].astype(BF16), w_down[l].astype(BF16))
        ffn = _combine(idx3, yo, s)
        h = _add_ln(h1, ffn.reshape(t, d), ln2_g[l], ln2_b[l], tm)
    return h.reshape(b, s, d)
```

```python
import functools
import math

import jax
import jax.numpy as jnp
from jax import lax
from jax.experimental import pallas as pl
from jax.experimental.pallas import tpu as pltpu

F32 = jnp.float32
BF16 = jnp.bfloat16
HIGHEST = lax.Precision.HIGHEST

D_MODEL = 1024
DEPTH = 2
ATTN_HEADS = 4
ATTN_QK_DIM = 64
ATTN_V_DIM = 128
DN_HEADS = 4
DN_HEAD_DIM = 128
CONV_WIDTH = 5
CHUNK = 64
N_EXPERTS = 16
EXPERT_HIDDEN = 1024
CAPACITY_FACTOR = 2
DEEPNORM_ALPHA = (2 * DEPTH) ** 0.25
LN_EPS = 1e-5
RMS_EPS = 1e-6

LANES = 128
GATES_W = 2 * D_MODEL
MAIN_W = GATES_W + 3 * 512 + 3 * 512 + 512
COL_AQ = GATES_W // LANES
COL_AK = COL_AQ + 4
COL_AV = COL_AK + 4
COL_DQ = COL_AV + 4
COL_DK = COL_DQ + 4
COL_DV = COL_DK + 4
COL_DG = COL_DV + 4

GDN_TILE = 256
ATTN_TILE = 512
BF16_INT = 256
VMEM_LIMIT = 56 * 1024 * 1024


def _cp(sem):
    return pltpu.CompilerParams(dimension_semantics=sem, vmem_limit_bytes=VMEM_LIMIT)


def _sigmoid(x):
    return 1.0 / (1.0 + jnp.exp(-x))


def _silu(x):
    return x * _sigmoid(x)


def _layer_norm(y, g, b):
    mu = jnp.mean(y, axis=-1, keepdims=True)
    yc = y - mu
    var = jnp.mean(yc * yc, axis=-1, keepdims=True)
    return yc * lax.rsqrt(var + LN_EPS) * g + b


def _ln_kernel(x_ref, g_ref, b_ref, o_ref):
    o_ref[...] = _layer_norm(x_ref[...], g_ref[...], b_ref[...])


def _add_ln_kernel(h_ref, f_ref, g_ref, b_ref, o_ref):
    o_ref[...] = _layer_norm(DEEPNORM_ALPHA * h_ref[...] + f_ref[...], g_ref[...], b_ref[...])


def _ln(x2, g, b, tm):
    t, d = x2.shape
    row = pl.BlockSpec((tm, d), lambda i: (i, 0))
    vec = pl.BlockSpec((1, d), lambda i: (0, 0))
    return pl.pallas_call(
        _ln_kernel, grid=(t // tm,), in_specs=[row, vec, vec], out_specs=row,
        out_shape=jax.ShapeDtypeStruct((t, d), F32), compiler_params=_cp(("parallel",)),
        name="ln0")(x2, g.reshape(1, d), b.reshape(1, d))


def _add_ln(h2, f2, g, b, tm):
    t, d = h2.shape
    row = pl.BlockSpec((tm, d), lambda i: (i, 0))
    vec = pl.BlockSpec((1, d), lambda i: (0, 0))
    return pl.pallas_call(
        _add_ln_kernel, grid=(t // tm,), in_specs=[row, row, vec, vec], out_specs=row,
        out_shape=jax.ShapeDtypeStruct((t, d), F32), compiler_params=_cp(("parallel",)),
        name="add_ln")(h2, f2, g.reshape(1, d), b.reshape(1, d))


def _inproj_kernel(h_ref, wm_ref, ws_ref, alog_ref, dtb_ref, mf_ref, mb_ref, om_ref, os_ref, ot_ref,
                   *, ncol):
    h = h_ref[...]
    hb = h.astype(BF16)
    for c in range(0, MAIN_W, ncol):
        om_ref[:, c:c + ncol] = jnp.dot(hb, wm_ref[:, c:c + ncol],
                                        preferred_element_type=F32).astype(BF16)
    sm = jnp.dot(h, ws_ref[...], preferred_element_type=F32, precision=HIGHEST)
    lane = lax.broadcasted_iota(jnp.int32, sm.shape, 1)
    beta = _sigmoid(sm)
    z = sm + dtb_ref[...]
    softplus = jnp.maximum(z, 0.0) + jnp.log(1.0 + jnp.exp(-jnp.abs(z)))
    g = jnp.where((lane >= 8) & (lane < 16), -jnp.exp(alog_ref[...]) * softplus, 0.0)
    gf = jnp.dot(mf_ref[...], g, preferred_element_type=F32, precision=HIGHEST)
    gb = jnp.dot(mb_ref[...], g, preferred_element_type=F32, precision=HIGHEST)
    sc = jnp.where(lane < 8, beta, jnp.where(lane < 12, gf, gb))
    os_ref[...] = sc
    sct = sc.T
    for n in range(sc.shape[0] // GDN_TILE):
        ot_ref[n] = sct[0:16, n * GDN_TILE:(n + 1) * GDN_TILE]


def _inproj(h2, wm, ws, alog, dtb, mf, mb, tm):
    t, d = h2.shape
    const = lambda shape: pl.BlockSpec(shape, lambda i: (0, 0))
    return pl.pallas_call(
        functools.partial(_inproj_kernel, ncol=512),
        grid=(t // tm,),
        in_specs=[pl.BlockSpec((tm, d), lambda i: (i, 0)), const((d, MAIN_W)), const((d, LANES)),
                  const((1, LANES)), const((1, LANES)), const((tm, tm)), const((tm, tm))],
        out_specs=[pl.BlockSpec((tm, MAIN_W), lambda i: (i, 0)),
                   pl.BlockSpec((tm, LANES), lambda i: (i, 0)),
                   pl.BlockSpec((tm // GDN_TILE, 16, GDN_TILE), lambda i: (i, 0, 0))],
        out_shape=[jax.ShapeDtypeStruct((t, MAIN_W), BF16), jax.ShapeDtypeStruct((t, LANES), F32),
                   jax.ShapeDtypeStruct((t // GDN_TILE, 16, GDN_TILE), F32)],
        compiler_params=_cp(("parallel",)), name="inproj")(h2, wm, ws, alog, dtb, mf, mb)


def _attn_kernel(slope_ref, lamp_ref, g_ref, q_ref, k_ref, v_ref, o_ref, ka_s, va_s,
                 *, tile, seq, lambda_init):
    hh = pl.program_id(1)
    qi = pl.program_id(2)
    slope = slope_ref[hh]
    nk = seq // tile

    @pl.when(qi == 0)
    def _():
        lane = lax.broadcasted_iota(jnp.int32, (seq, LANES), 1)
        col = lax.broadcasted_iota(jnp.int32, (seq, LANES), 0) & (tile - 1)
        col_lo = (col & (BF16_INT - 1)).astype(F32)
        col_hi = (col & -BF16_INT).astype(F32)
        ka_s[:, :LANES] = k_ref[...]
        ka_s[:, LANES:] = jnp.where(lane < 3, 1.0, jnp.where(lane == 3, col_lo, jnp.where(
            lane == 4, col_hi, 0.0))).astype(BF16)
        va_s[:, :LANES] = v_ref[...]
        va_s[:, LANES:] = jnp.where(lane == 0, 1.0, 0.0).astype(BF16)

    lp = lamp_ref[...]
    lam = (jnp.exp(jnp.sum(lp[0:1] * lp[1:2], axis=-1, keepdims=True))
           - jnp.exp(jnp.sum(lp[2:3] * lp[3:4], axis=-1, keepdims=True)) + lambda_init)
    q = q_ref[...] * (ATTN_QK_DIM ** -0.5)
    lane = lax.broadcasted_iota(jnp.int32, q.shape, 1)
    zero = jnp.zeros_like(q)
    q1 = jnp.where(lane < ATTN_QK_DIM, q, zero)
    q2 = jnp.where(lane >= ATTN_QK_DIM, q, zero)
    row = lax.broadcasted_iota(jnp.int32, q.shape, 0)
    row_lo = (row & (BF16_INT - 1)).astype(F32)
    row_hi = (row & -BF16_INT).astype(F32)
    nt = (((1,), (1,)), ((), ()))

    def q_aug(sgn, dabs):
        aug = jnp.where(lane == 0, (-sgn * slope) * row_lo, jnp.where(
            lane == 1, (-sgn * slope) * row_hi, jnp.where(
                lane == 2, -slope * tile * dabs, jnp.where(
                    (lane == 3) | (lane == 4), sgn * slope, 0.0)))).astype(BF16)
        return jnp.concatenate([q1, aug], axis=1), jnp.concatenate([q2, aug], axis=1)

    def tile_step(j, carry, qa1, qa2, bias):
        m1, a1, m2, a2 = carry
        rows = pl.ds(pl.multiple_of(j * tile, tile), tile)
        k = ka_s[rows, :]
        v = va_s[rows, :]

        def one(qa, m, a):
            s = lax.dot_general(qa, k, nt, preferred_element_type=F32)
            if bias is not None:
                s = s - bias
            m_new = jnp.maximum(m, jnp.max(s, axis=-1, keepdims=True))
            alpha = jnp.exp(m - m_new)
            p = jnp.exp((s - m_new).astype(BF16))
            return m_new, alpha * a + jnp.dot(p, v, preferred_element_type=F32)

        m1, a1 = one(qa1, m1, a1)
        m2, a2 = one(qa2, m2, a2)
        return m1, a1, m2, a2

    def left(j, carry):
        qa1, qa2 = q_aug(1.0, (qi - j).astype(F32))
        return tile_step(j, carry, qa1, qa2, None)

    def right(j, carry):
        qa1, qa2 = q_aug(-1.0, (j - qi).astype(F32))
        return tile_step(j, carry, qa1, qa2, None)

    neg = jnp.full((tile, 1), -1e30, F32)
    za = jnp.zeros((tile, 2 * LANES), F32)
    carry = lax.fori_loop(0, qi, left, (neg, za, neg, za))
    absrel = jnp.abs(lax.broadcasted_iota(jnp.int32, (tile, tile), 0)
                     - lax.broadcasted_iota(jnp.int32, (tile, tile), 1)).astype(F32)
    zaug = jnp.zeros_like(q)
    carry = tile_step(qi, carry, jnp.concatenate([q1, zaug], axis=1),
                      jnp.concatenate([q2, zaug], axis=1), slope * absrel)
    m1, a1, m2, a2 = lax.fori_loop(qi + 1, nk, right, carry)
    o = (a1[:, :ATTN_V_DIM] / a1[:, ATTN_V_DIM:ATTN_V_DIM + 1]
         - lam * (a2[:, :ATTN_V_DIM] / a2[:, ATTN_V_DIM:ATTN_V_DIM + 1]))
    r = lax.rsqrt(jnp.mean(o * o, axis=-1, keepdims=True) + RMS_EPS)
    o_ref[...] = (o * r * g_ref[...] * (1.0 - lambda_init)).astype(o_ref.dtype)


def _attention(pm, slopes, lamp, norm_g, lambda_init, tile):
    b, s, _ = pm.shape
    kern = functools.partial(_attn_kernel, tile=tile, seq=s, lambda_init=lambda_init)
    return pl.pallas_call(
        kern, grid=(b, ATTN_HEADS, s // tile),
        in_specs=[pl.BlockSpec(memory_space=pltpu.SMEM),
                  pl.BlockSpec((4, ATTN_QK_DIM), lambda bi, h, i: (0, 0)),
                  pl.BlockSpec((1, ATTN_V_DIM), lambda bi, h, i: (0, 0)),
                  pl.BlockSpec((None, tile, LANES), lambda bi, h, i: (bi, i, COL_AQ + h)),
                  pl.BlockSpec((None, s, LANES), lambda bi, h, i: (bi, 0, COL_AK + h)),
                  pl.BlockSpec((None, s, LANES), lambda bi, h, i: (bi, 0, COL_AV + h))],
        out_specs=pl.BlockSpec((None, tile, LANES), lambda bi, h, i: (bi, i, h)),
        out_shape=jax.ShapeDtypeStruct((b, s, ATTN_HEADS * ATTN_V_DIM), BF16),
        scratch_shapes=[pltpu.VMEM((s, 2 * LANES), BF16), pltpu.VMEM((s, 2 * LANES), BF16)],
        compiler_params=_cp(("parallel", "parallel", "arbitrary")), name="diff_attn",
    )(slopes, lamp, norm_g.reshape(1, ATTN_V_DIM), pm, pm, pm)


def _lane_pick(x, idx):
    lane = lax.broadcasted_iota(jnp.int32, x.shape, 1)
    return jnp.sum(jnp.where(lane == idx, x, 0.0), axis=1, keepdims=True)


def _unit_tri_inverse(lm, eye):
    p = -lm
    inv = eye + p
    for _ in range(int(math.log2(CHUNK)) - 1):
        p16 = p.astype(BF16)
        p = jnp.dot(p16, p16, preferred_element_type=F32)
        inv = inv + jnp.dot(inv.astype(BF16), p.astype(BF16), preferred_element_type=F32)
    return inv


def _gdn_kernel(qr_ref, kr_ref, vr_ref, gt_ref, sm_ref, smt_ref, cw_ref, ng_ref, o_ref,
                q_s, k_s, v_s, qw_s, m_s, n_s, st_s, dec_s, of_s, ob_s, *, seq):
    hh = pl.program_id(1)
    nchunk = seq // CHUNK
    per = GDN_TILE // CHUNK
    hd = DN_HEAD_DIM
    row = lax.broadcasted_iota(jnp.int32, (seq, hd), 0)

    def conv_silu(ref, widx):
        u = ref[...].astype(F32)
        w = cw_ref[widx]
        acc = jnp.zeros_like(u)
        for j in range(CONV_WIDTH):
            d = j - CONV_WIDTH // 2
            if d == 0:
                sh = u
            else:
                sh = pltpu.roll(u, (-d) % seq, axis=0)
                ok = (row + d >= 0) & (row + d < seq)
                sh = jnp.where(ok, sh, 0.0)
            acc = acc + sh * w[j:j + 1, :]
        return _silu(acc)

    def l2n(x):
        return x * lax.rsqrt(jnp.sum(x * x, axis=-1, keepdims=True) + RMS_EPS)

    q_s[...] = l2n(conv_silu(qr_ref, 0)) * (hd ** -0.5)
    k_s[...] = l2n(conv_silu(kr_ref, 1))
    v_s[...] = conv_silu(vr_ref, 2)

    ti = lax.broadcasted_iota(jnp.int32, (GDN_TILE, GDN_TILE), 0)
    tj = lax.broadcasted_iota(jnp.int32, (GDN_TILE, GDN_TILE), 1)
    same = (ti // CHUNK) == (tj // CHUNK)
    incl = (same & (tj <= ti), same & (tj >= ti))
    offdiag = ti != tj
    eye = jnp.where(offdiag, 0.0, 1.0)
    nt = (((1,), (1,)), ((), ()))

    def local(n, _):
        rows = pl.ds(pl.multiple_of(n * GDN_TILE, GDN_TILE), GDN_TILE)
        qc = q_s[rows, :]
        kc = k_s[rows, :]
        vc = v_s[rows, :]
        smc = sm_ref[rows, :]
        k16 = kc.astype(BF16)
        kk = lax.dot_general(k16, k16, nt, preferred_element_type=F32)
        qk = lax.dot_general(qc.astype(BF16), k16, nt, preferred_element_type=F32)
        for d in range(2):
            o_s = of_s if d == 0 else ob_s
            beta = _lane_pick(smc, d * DN_HEADS + hh)
            gcol = _lane_pick(smc, 8 + d * DN_HEADS + hh)
            grow = smt_ref[n, pl.ds(8 + d * DN_HEADS + hh, 1), :]
            decay = jnp.where(incl[d], jnp.exp(jnp.where(incl[d], gcol - grow, 0.0)), 0.0)
            lm = jnp.where(offdiag, beta * kk * decay, 0.0)
            inv = _unit_tri_inverse(lm, eye)
            eg = jnp.exp(gcol)
            rhs = jnp.concatenate([vc * beta, kc * (beta * eg)], axis=1)
            sol = jnp.dot(inv.astype(BF16), rhs.astype(BF16), preferred_element_type=F32)
            attn = (qk * decay).astype(BF16)
            auw = jnp.dot(attn, sol.astype(BF16), preferred_element_type=F32)
            o_s[rows, :] = auw[:, :hd]
            qw_s[d, rows, :] = (qc * eg - auw[:, hd:]).astype(BF16)
            last = CHUNK - 1 if d == 0 else 0
            glast = jnp.concatenate(
                [jnp.broadcast_to(gcol[m * CHUNK + last:m * CHUNK + last + 1, :], (CHUNK, 1))
                 for m in range(per)], axis=0)
            kd = kc * jnp.exp(glast - gcol)
            wu16 = jnp.concatenate([sol[:, hd:], sol[:, :hd]], axis=1).astype(BF16)
            for m in range(per):
                c = n * per + m
                blk = slice(m * CHUNK, (m + 1) * CHUNK)
                mats = pl.ds(pl.multiple_of(c * hd, hd), hd)
                mn = jnp.dot(kd[blk].T.astype(BF16), wu16[blk], preferred_element_type=F32)
                m_s[d, mats, :] = (-mn[:, :hd]).astype(BF16)
                n_s[d, mats, :] = mn[:, hd:].astype(BF16)
                dec_s[d, pl.ds(c, 1), :] = jnp.broadcast_to(
                    jnp.exp(gcol[m * CHUNK + last:m * CHUNK + last + 1, :]), (1, hd))
        return 0

    lax.fori_loop(0, seq // GDN_TILE, local, 0, unroll=2)

    def scan(n, states):
        new_states = []
        for d in range(2):
            c = n if d == 0 else nchunk - 1 - n
            mats = pl.ds(pl.multiple_of(c * hd, hd), hd)
            st = states[d]
            s16 = st.astype(BF16)
            st_s[d, mats, :] = s16
            st = (st * dec_s[d, pl.ds(c, 1), :] + n_s[d, mats, :].astype(F32)
                  + jnp.dot(m_s[d, mats, :], s16, preferred_element_type=F32))
            new_states.append(st)
        return tuple(new_states)

    z = jnp.zeros((hd, hd), F32)
    lax.fori_loop(0, nchunk, scan, (z, z))

    def emit(c, _):
        rows = pl.ds(pl.multiple_of(c * CHUNK, CHUNK), CHUNK)
        mats = pl.ds(pl.multiple_of(c * hd, hd), hd)
        for d in range(2):
            o_s = of_s if d == 0 else ob_s
            o_s[rows, :] = o_s[rows, :] + jnp.dot(qw_s[d, rows, :], st_s[d, mats, :],
                                                  preferred_element_type=F32)
        return 0

    lax.fori_loop(0, nchunk, emit, 0, unroll=4)

    o = of_s[...] + ob_s[...]
    r = lax.rsqrt(jnp.mean(o * o, axis=-1, keepdims=True) + RMS_EPS)
    o_ref[...] = (o * r * ng_ref[...] * _silu(gt_ref[...].astype(F32))).astype(o_ref.dtype)


def _gdn(pm, sm, smt, cw, norm_g):
    b, s, _ = pm.shape
    nchunk = s // CHUNK
    ntile = s // GDN_TILE
    col = lambda c0: pl.BlockSpec((None, s, LANES), lambda bi, h: (bi, 0, c0 + h))
    hd = DN_HEAD_DIM
    return pl.pallas_call(
        functools.partial(_gdn_kernel, seq=s), grid=(b, DN_HEADS),
        in_specs=[col(COL_DQ), col(COL_DK), col(COL_DV), col(COL_DG),
                  pl.BlockSpec((None, s, LANES), lambda bi, h: (bi, 0, 0)),
                  pl.BlockSpec((None, ntile, 16, GDN_TILE), lambda bi, h: (bi, 0, 0, 0)),
                  pl.BlockSpec((3, None, CONV_WIDTH, hd), lambda bi, h: (0, h, 0, 0)),
                  pl.BlockSpec((1, hd), lambda bi, h: (0, 0))],
        out_specs=pl.BlockSpec((None, s, LANES), lambda bi, h: (bi, 0, h)),
        out_shape=jax.ShapeDtypeStruct((b, s, DN_HEADS * hd), BF16),
        scratch_shapes=[pltpu.VMEM((s, hd), F32), pltpu.VMEM((s, hd), F32), pltpu.VMEM((s, hd), F32),
                        pltpu.VMEM((2, s, hd), BF16), pltpu.VMEM((2, nchunk * hd, hd), BF16),
                        pltpu.VMEM((2, nchunk * hd, hd), BF16), pltpu.VMEM((2, nchunk * hd, hd), BF16),
                        pltpu.VMEM((2, nchunk, hd), F32),
                        pltpu.VMEM((s, hd), F32), pltpu.VMEM((s, hd), F32)],
        compiler_params=_cp(("parallel", "parallel")), name="gdn",
    )(pm, pm, pm, pm, sm, smt, cw, norm_g.reshape(1, hd))


def _outproj_kernel(h_ref, oa_ref, od_ref, ga_ref, gd_ref, wpa_ref, wpd_ref, wo_ref, g_ref, b_ref,
                    wr_ref, h1_ref, aff_ref):
    pa = jnp.dot(oa_ref[...], wpa_ref[...], preferred_element_type=F32)
    pd = jnp.dot(od_ref[...], wpd_ref[...], preferred_element_type=F32)
    merged = _sigmoid(ga_ref[...].astype(F32)) * pa + _sigmoid(gd_ref[...].astype(F32)) * pd
    mix = jnp.dot(merged.astype(BF16), wo_ref[...], preferred_element_type=F32)
    h1 = _layer_norm(DEEPNORM_ALPHA * h_ref[...] + mix, g_ref[...], b_ref[...])
    h1_ref[...] = h1
    logits = jnp.dot(h1, wr_ref[...], preferred_element_type=F32, precision=HIGHEST)
    lane = lax.broadcasted_iota(jnp.int32, logits.shape, 1)
    logits = jnp.where(lane < N_EXPERTS, logits, -1e30)
    e = jnp.exp(logits - jnp.max(logits, axis=-1, keepdims=True))
    aff_ref[...] = e / jnp.sum(e, axis=-1, keepdims=True)


def _outproj(h2, oa2, od2, pm2, wpa, wpd, wo, g, b, wr, tm):
    t, d = h2.shape
    row = lambda w, c=0: pl.BlockSpec((tm, w), lambda i: (i, c))
    const = lambda shape: pl.BlockSpec(shape, lambda i: (0, 0))
    return pl.pallas_call(
        _outproj_kernel, grid=(t // tm,),
        in_specs=[row(d), row(512), row(512), row(d, 0), row(d, 1),
                  const((512, d)), const((512, d)), const((d, d)), const((1, d)), const((1, d)),
                  const((d, LANES))],
        out_specs=[row(d), row(LANES)],
        out_shape=[jax.ShapeDtypeStruct((t, d), F32), jax.ShapeDtypeStruct((t, LANES), F32)],
        compiler_params=_cp(("parallel",)), name="outproj",
    )(h2, oa2, od2, pm2, pm2, wpa, wpd, wo, g.reshape(1, d), b.reshape(1, d), wr)


def _topk_kernel(aff_ref, tri_ref, idx_ref, pin_s, *, seq, cap, blk):
    aff = aff_ref[...]
    bits = pltpu.bitcast(aff, jnp.int32)
    ones = lambda cond: jnp.where(cond, 1.0, 0.0)

    def refine(i, thr):
        cand = thr | (jnp.int32(1) << (30 - i))
        cnt = jnp.sum(ones(bits >= cand), axis=0, keepdims=True)
        return jnp.where(cnt >= cap, cand, thr)

    thr = lax.fori_loop(0, 31, refine, jnp.zeros((1, LANES), jnp.int32))
    gt = ones(bits > thr)
    eq = ones(bits == thr)
    need = cap - jnp.sum(gt, axis=0, keepdims=True)

    def prefix(mask):
        carry = jnp.zeros((1, LANES), F32)
        for r in range(0, seq, blk):
            m = mask[r:r + blk, :]
            loc = jnp.dot(tri_ref[...], m.astype(BF16), preferred_element_type=F32)
            pin_s[r:r + blk, :] = loc + carry
            carry = carry + jnp.sum(m, axis=0, keepdims=True)
        return pin_s[...]

    eq_rank = prefix(eq)
    sel = jnp.maximum(gt, eq * ones(eq_rank <= need))
    prefix(sel)
    slot = lax.broadcasted_iota(jnp.int32, (blk, cap), 1).astype(F32)
    for e in range(N_EXPERTS):
        acc = jnp.zeros((1, cap), F32)
        for r in range(0, seq, blk):
            col = pin_s[r:r + blk, e:e + 1]
            acc = acc + jnp.sum(ones(col <= slot), axis=0, keepdims=True)
        idx_ref[e:e + 1, :] = acc.astype(jnp.int32)


def _topk(aff3, tri, cap, blk):
    b, s, _ = aff3.shape
    return pl.pallas_call(
        functools.partial(_topk_kernel, seq=s, cap=cap, blk=blk), grid=(b,),
        in_specs=[pl.BlockSpec((None, s, LANES), lambda bi: (bi, 0, 0)),
                  pl.BlockSpec((blk, blk), lambda bi: (0, 0))],
        out_specs=pl.BlockSpec((None, N_EXPERTS, cap), lambda bi: (bi, 0, 0)),
        out_shape=jax.ShapeDtypeStruct((b, N_EXPERTS, cap), jnp.int32),
        scratch_shapes=[pltpu.VMEM((s, LANES), F32)],
        compiler_params=_cp(("parallel",)), name="topk")(aff3, tri)


def _gather_kernel(idx_ref, x_ref, aff_ref, xg_ref, gg_ref, xs_s, *, cap):
    def body(s, _):
        t = idx_ref[0, s]
        xs_s[pl.ds(s, 1), :] = x_ref[pl.ds(t, 1), :]
        gg_ref[pl.ds(s, 1), :] = aff_ref[pl.ds(t, 1), :]
        return 0

    lax.fori_loop(0, cap, body, 0, unroll=8)
    xg_ref[...] = xs_s[...].astype(xg_ref.dtype)


def _gather(idx3, h3, aff3, cap):
    b, s, d = h3.shape
    return pl.pallas_call(
        functools.partial(_gather_kernel, cap=cap), grid=(b, N_EXPERTS),
        in_specs=[pl.BlockSpec((None, 1, cap), lambda bi, e: (bi * N_EXPERTS + e, 0, 0),
                               memory_space=pltpu.SMEM),
                  pl.BlockSpec((None, s, d), lambda bi, e: (bi, 0, 0)),
                  pl.BlockSpec((None, s, LANES), lambda bi, e: (bi, 0, 0))],
        out_specs=[pl.BlockSpec((None, None, cap, d), lambda bi, e: (bi, e, 0, 0)),
                   pl.BlockSpec((None, None, cap, LANES), lambda bi, e: (bi, e, 0, 0))],
        out_shape=[jax.ShapeDtypeStruct((b, N_EXPERTS, cap, d), BF16),
                   jax.ShapeDtypeStruct((b, N_EXPERTS, cap, LANES), F32)],
        scratch_shapes=[pltpu.VMEM((cap, d), F32)],
        compiler_params=_cp(("parallel", "arbitrary")), name="moe_gather")(idx3, h3, aff3)


def _ffn_kernel(xg_ref, gg_ref, wg_ref, wu_ref, wd_ref, yo_ref):
    e = pl.program_id(0)
    x = xg_ref[...]
    a = jnp.dot(x, wg_ref[...], preferred_element_type=F32)
    u = jnp.dot(x, wu_ref[...], preferred_element_type=F32)
    hid = (_silu(a) * u).astype(BF16)
    y = jnp.dot(hid, wd_ref[...], preferred_element_type=F32)
    yo_ref[...] = y * _lane_pick(gg_ref[...], e)


def _ffn(xg, gg, wg, wu, wd):
    b, ne, cap, d = xg.shape
    f = wg.shape[-1]
    return pl.pallas_call(
        _ffn_kernel, grid=(ne, b),
        in_specs=[pl.BlockSpec((None, None, cap, d), lambda e, bi: (bi, e, 0, 0)),
                  pl.BlockSpec((None, None, cap, LANES), lambda e, bi: (bi, e, 0, 0)),
                  pl.BlockSpec((None, d, f), lambda e, bi: (e, 0, 0)),
                  pl.BlockSpec((None, d, f), lambda e, bi: (e, 0, 0)),
                  pl.BlockSpec((None, f, d), lambda e, bi: (e, 0, 0))],
        out_specs=pl.BlockSpec((None, None, cap, d), lambda e, bi: (bi, e, 0, 0)),
        out_shape=jax.ShapeDtypeStruct((b, ne, cap, d), F32),
        compiler_params=_cp(("parallel", "parallel")), name="moe_ffn")(xg, gg, wg, wu, wd)


def _combine_kernel(idx_ref, yo_ref, acc_ref, *, cap):
    @pl.when(pl.program_id(1) == 0)
    def _():
        acc_ref[...] = jnp.zeros_like(acc_ref)

    def body(s, _):
        t = idx_ref[0, s]
        acc_ref[pl.ds(t, 1), :] = acc_ref[pl.ds(t, 1), :] + yo_ref[pl.ds(s, 1), :]
        return 0

    lax.fori_loop(0, cap, body, 0, unroll=8)


def _combine(idx3, yo, s):
    b, ne, cap, d = yo.shape
    return pl.pallas_call(
        functools.partial(_combine_kernel, cap=cap), grid=(b, ne),
        in_specs=[pl.BlockSpec((None, 1, cap), lambda bi, e: (bi * N_EXPERTS + e, 0, 0),
                               memory_space=pltpu.SMEM),
                  pl.BlockSpec((None, None, cap, d), lambda bi, e: (bi, e, 0, 0))],
        out_specs=pl.BlockSpec((None, s, d), lambda bi, e: (bi, 0, 0)),
        out_shape=jax.ShapeDtypeStruct((b, s, d), F32),
        compiler_params=_cp(("parallel", "arbitrary")), name="moe_combine")(idx3, yo)


def _block_tri(n, lower):
    i = jnp.arange(n)[:, None]
    j = jnp.arange(n)[None, :]
    same = (i // CHUNK) == (j // CHUNK)
    return (same & ((j <= i) if lower else (j >= i))).astype(F32)


def kernel(x, ln0_g, ln0_b, w_in, lam_q1, lam_k1, lam_q2, lam_k2, attn_norm_g, conv_w, a_log, dt_bias,
           dn_norm_g, w_pa, w_pd, w_o, ln1_g, ln1_b, w_router, w_gate, w_up, w_down, ln2_g, ln2_b):
    b, s, d = x.shape
    t = b * s
    cap = CAPACITY_FACTOR * s // N_EXPERTS
    tm = min(512, s)
    blk = min(512, s)

    slopes = 2.0 ** (-8.0 * jnp.arange(1, ATTN_HEADS + 1, dtype=F32) / ATTN_HEADS)
    mf = _block_tri(tm, True)
    mb = _block_tri(tm, False)
    tri = jnp.tril(jnp.ones((blk, blk), BF16))
    pad16 = lambda v: jnp.pad(v.reshape(1, -1), ((0, 0), (8, LANES - 16)))

    h = _ln(x.reshape(t, d), ln0_g, ln0_b, tm)
    for l in range(DEPTH):
        lambda_init = 0.8 - 0.6 * math.exp(-0.3 * l)
        w = w_in[l]
        wm = jnp.concatenate([w[:, 3600:], w[:, :3584]], axis=1).astype(BF16)
        ws = jnp.pad(w[:, 3584:3600], ((0, 0), (0, LANES - 16)))
        pm, sm, smt = _inproj(h, wm, ws, pad16(a_log[l]), pad16(dt_bias[l]), mf, mb, tm)
        pm3 = pm.reshape(b, s, MAIN_W)
        lamp = jnp.stack([lam_q1[l], lam_k1[l], lam_q2[l], lam_k2[l]])
        oa = _attention(pm3, slopes, lamp, attn_norm_g[l], lambda_init, min(ATTN_TILE, s))
        cw = conv_w[l].reshape(CONV_WIDTH, 3, DN_HEADS, DN_HEAD_DIM).transpose(1, 2, 0, 3)
        od = _gdn(pm3, sm.reshape(b, s, LANES), smt.reshape(b, s // GDN_TILE, 16, GDN_TILE), cw,
                  dn_norm_g[l])
        wr = jnp.pad(w_router[l], ((0, 0), (0, LANES - N_EXPERTS)))
        h1, aff = _outproj(h, oa.reshape(t, -1), od.reshape(t, -1), pm,
                           w_pa[l].astype(BF16), w_pd[l].astype(BF16), w_o[l].astype(BF16),
                           ln1_g[l], ln1_b[l], wr, tm)
        aff3 = aff.reshape(b, s, LANES)
        idx = _topk(aff3, tri, cap, blk)
        idx3 = idx.reshape(b * N_EXPERTS, 1, cap)
        xg, gg = _gather(idx3, h1.reshape(b, s, d), aff3, cap)
        yo = _ffn(xg, gg, w_gate[l].astype(BF16), w_up[l].astype(BF16), w_down[l].astype(BF16))
        ffn = _combine(idx3, yo, s)
        h = _add_ln(h1, ffn.reshape(t, d), ln2_g[l], ln2_b[l], tm)
    return h.reshape(b, s, d)
```

```python
import functools
import math

import jax
import jax.numpy as jnp
from jax import lax
from jax.experimental import pallas as pl
from jax.experimental.pallas import tpu as pltpu

F32 = jnp.float32
BF16 = jnp.bfloat16
HIGHEST = lax.Precision.HIGHEST

D_MODEL = 1024
DEPTH = 2
ATTN_HEADS = 4
ATTN_QK_DIM = 64
ATTN_V_DIM = 128
DN_HEADS = 4
DN_HEAD_DIM = 128
CONV_WIDTH = 5
CHUNK = 64
N_EXPERTS = 16
EXPERT_HIDDEN = 1024
CAPACITY_FACTOR = 2
DEEPNORM_ALPHA = (2 * DEPTH) ** 0.25
LN_EPS = 1e-5
RMS_EPS = 1e-6

LANES = 128
GATES_W = 2 * D_MODEL
MAIN_W = GATES_W + 3 * 512 + 3 * 512 + 512
COL_AQ = GATES_W // LANES
COL_AK = COL_AQ + 4
COL_AV = COL_AK + 4
COL_DQ = COL_AV + 4
COL_DK = COL_DQ + 4
COL_DV = COL_DK + 4
COL_DG = COL_DV + 4

GDN_TILE = 256
ATTN_TILE = 512
ATTN_ROW_SPLITS = 2
BF16_INT = 256
VMEM_LIMIT = 56 * 1024 * 1024


def _cp(sem):
    return pltpu.CompilerParams(dimension_semantics=sem, vmem_limit_bytes=VMEM_LIMIT)


def _sigmoid(x):
    return 1.0 / (1.0 + jnp.exp(-x))


def _silu(x):
    return x * _sigmoid(x)


def _split2(x):
    hi = x.astype(BF16)
    return hi, (x - hi.astype(F32)).astype(BF16)


def _dot3(a, b):
    a_hi, a_lo = _split2(a)
    b_hi, b_lo = _split2(b)
    return (jnp.dot(a_hi, b_hi, preferred_element_type=F32)
            + jnp.dot(a_hi, b_lo, preferred_element_type=F32)
            + jnp.dot(a_lo, b_hi, preferred_element_type=F32))


def _dot_mask(m16, x):
    x_hi, x_mid = _split2(x)
    r = x - x_hi.astype(F32) - x_mid.astype(F32)
    return (jnp.dot(m16, x_hi, preferred_element_type=F32)
            + jnp.dot(m16, x_mid, preferred_element_type=F32)
            + jnp.dot(m16, r.astype(BF16), preferred_element_type=F32))


def _layer_norm(y, g, b):
    mu = jnp.mean(y, axis=-1, keepdims=True)
    yc = y - mu
    var = jnp.mean(yc * yc, axis=-1, keepdims=True)
    return yc * lax.rsqrt(var + LN_EPS) * g + b


def _ln_kernel(x_ref, g_ref, b_ref, o_ref):
    o_ref[...] = _layer_norm(x_ref[...], g_ref[...], b_ref[...])


def _add_ln_kernel(h_ref, f_ref, g_ref, b_ref, o_ref):
    o_ref[...] = _layer_norm(DEEPNORM_ALPHA * h_ref[...] + f_ref[...], g_ref[...], b_ref[...])


def _ln(x2, g, b, tm):
    t, d = x2.shape
    row = pl.BlockSpec((tm, d), lambda i: (i, 0))
    vec = pl.BlockSpec((1, d), lambda i: (0, 0))
    return pl.pallas_call(
        _ln_kernel, grid=(t // tm,), in_specs=[row, vec, vec], out_specs=row,
        out_shape=jax.ShapeDtypeStruct((t, d), F32), compiler_params=_cp(("parallel",)),
        name="ln0")(x2, g.reshape(1, d), b.reshape(1, d))


def _add_ln(h2, f2, g, b, tm):
    t, d = h2.shape
    row = pl.BlockSpec((tm, d), lambda i: (i, 0))
    vec = pl.BlockSpec((1, d), lambda i: (0, 0))
    return pl.pallas_call(
        _add_ln_kernel, grid=(t // tm,), in_specs=[row, row, vec, vec], out_specs=row,
        out_shape=jax.ShapeDtypeStruct((t, d), F32), compiler_params=_cp(("parallel",)),
        name="add_ln")(h2, f2, g.reshape(1, d), b.reshape(1, d))


def _inproj_kernel(h_ref, wm_ref, ws_ref, alog_ref, dtb_ref, mf_ref, mb_ref, om_ref, os_ref, ot_ref,
                   *, ncol):
    h = h_ref[...]
    hb = h.astype(BF16)
    for c in range(0, MAIN_W, ncol):
        om_ref[:, c:c + ncol] = jnp.dot(hb, wm_ref[:, c:c + ncol],
                                        preferred_element_type=F32).astype(BF16)
    sm = _dot3(h, ws_ref[...])
    lane = lax.broadcasted_iota(jnp.int32, sm.shape, 1)
    beta = _sigmoid(sm)
    z = sm + dtb_ref[...]
    softplus = jnp.maximum(z, 0.0) + jnp.log(1.0 + jnp.exp(-jnp.abs(z)))
    g = jnp.where((lane >= 8) & (lane < 16), -jnp.exp(alog_ref[...]) * softplus, 0.0)
    gf = _dot_mask(mf_ref[...], g)
    gb = _dot_mask(mb_ref[...], g)
    sc = jnp.where(lane < 8, beta, jnp.where(lane < 12, gf, gb))
    os_ref[...] = sc
    sct = sc.T
    for n in range(sc.shape[0] // GDN_TILE):
        ot_ref[n] = sct[0:16, n * GDN_TILE:(n + 1) * GDN_TILE]


def _inproj(h2, wm, ws, alog, dtb, mf, mb, tm):
    t, d = h2.shape
    const = lambda shape: pl.BlockSpec(shape, lambda i: (0, 0))
    return pl.pallas_call(
        functools.partial(_inproj_kernel, ncol=512),
        grid=(t // tm,),
        in_specs=[pl.BlockSpec((tm, d), lambda i: (i, 0)), const((d, MAIN_W)), const((d, LANES)),
                  const((1, LANES)), const((1, LANES)), const((tm, tm)), const((tm, tm))],
        out_specs=[pl.BlockSpec((tm, MAIN_W), lambda i: (i, 0)),
                   pl.BlockSpec((tm, LANES), lambda i: (i, 0)),
                   pl.BlockSpec((tm // GDN_TILE, 16, GDN_TILE), lambda i: (i, 0, 0))],
        out_shape=[jax.ShapeDtypeStruct((t, MAIN_W), BF16), jax.ShapeDtypeStruct((t, LANES), F32),
                   jax.ShapeDtypeStruct((t // GDN_TILE, 16, GDN_TILE), F32)],
        compiler_params=_cp(("parallel",)), name="inproj")(h2, wm, ws, alog, dtb, mf, mb)


def _attn_kernel(slope_ref, lamp_ref, g_ref, q_ref, k_ref, v_ref, o_ref, ka_s, va_s,
                 *, tile, seq, lambda_init):
    hh = pl.program_id(1)
    qi = pl.program_id(2)
    slope = slope_ref[hh]
    nk = seq // tile

    @pl.when(qi == 0)
    def _():
        lane = lax.broadcasted_iota(jnp.int32, (seq, LANES), 1)
        col = lax.broadcasted_iota(jnp.int32, (seq, LANES), 0) & (tile - 1)
        col_lo = (col & (BF16_INT - 1)).astype(F32)
        col_hi = (col & -BF16_INT).astype(F32)
        ka_s[:, :LANES] = k_ref[...]
        ka_s[:, LANES:] = jnp.where(lane < 3, 1.0, jnp.where(lane == 3, col_lo, jnp.where(
            lane == 4, col_hi, 0.0))).astype(BF16)
        va_s[:, :LANES] = v_ref[...]
        va_s[:, LANES:] = jnp.where(lane == 0, 1.0, 0.0).astype(BF16)

    lp = lamp_ref[...]
    lam = (jnp.exp(jnp.sum(lp[0:1] * lp[1:2], axis=-1, keepdims=True))
           - jnp.exp(jnp.sum(lp[2:3] * lp[3:4], axis=-1, keepdims=True)) + lambda_init)
    q = q_ref[...] * (ATTN_QK_DIM ** -0.5)
    lane = lax.broadcasted_iota(jnp.int32, q.shape, 1)
    zero = jnp.zeros_like(q)
    q1 = jnp.where(lane < ATTN_QK_DIM, q, zero)
    q2 = jnp.where(lane >= ATTN_QK_DIM, q, zero)
    row = lax.broadcasted_iota(jnp.int32, q.shape, 0)
    row_lo = (row & (BF16_INT - 1)).astype(F32)
    row_hi = (row & -BF16_INT).astype(F32)
    nt = (((1,), (1,)), ((), ()))

    def q_aug(sgn, dabs):
        aug = jnp.where(lane == 0, (-sgn * slope) * row_lo, jnp.where(
            lane == 1, (-sgn * slope) * row_hi, jnp.where(
                lane == 2, -slope * tile * dabs, jnp.where(
                    (lane == 3) | (lane == 4), sgn * slope, 0.0)))).astype(BF16)
        return jnp.concatenate([q1, aug], axis=1), jnp.concatenate([q2, aug], axis=1)

    sub = tile // ATTN_ROW_SPLITS

    def tile_step(j, carry, qa1, qa2, bias):
        rows = pl.ds(pl.multiple_of(j * tile, tile), tile)
        k = ka_s[rows, :]
        v = va_s[rows, :]

        def one(qa, m, a, bias_rows):
            s = lax.dot_general(qa, k, nt, preferred_element_type=F32)
            if bias_rows is not None:
                s = s - bias_rows
            m_new = jnp.maximum(m, jnp.max(s, axis=-1, keepdims=True))
            alpha = jnp.exp(m - m_new)
            p = jnp.exp((s - m_new).astype(BF16))
            return m_new, alpha * a + jnp.dot(p, v, preferred_element_type=F32)

        out = []
        for g, (qa, n) in enumerate([(qa1, 0), (qa2, 1)] * ATTN_ROW_SPLITS):
            r0 = (g // 2) * sub
            m, a = carry[g]
            out.append(one(qa[r0:r0 + sub], m, a, None if bias is None else bias[r0:r0 + sub]))
        return tuple(out)

    def left(j, carry):
        qa1, qa2 = q_aug(1.0, (qi - j).astype(F32))
        return tile_step(j, carry, qa1, qa2, None)

    def right(j, carry):
        qa1, qa2 = q_aug(-1.0, (j - qi).astype(F32))
        return tile_step(j, carry, qa1, qa2, None)

    neg = jnp.full((sub, 1), -1e30, F32)
    za = jnp.zeros((sub, 2 * LANES), F32)
    carry = lax.fori_loop(0, qi, left, ((neg, za),) * (2 * ATTN_ROW_SPLITS))
    absrel = jnp.abs(lax.broadcasted_iota(jnp.int32, (tile, tile), 0)
                     - lax.broadcasted_iota(jnp.int32, (tile, tile), 1)).astype(F32)
    zaug = jnp.zeros_like(q)
    carry = tile_step(qi, carry, jnp.concatenate([q1, zaug], axis=1),
                      jnp.concatenate([q2, zaug], axis=1), slope * absrel)
    carry = lax.fori_loop(qi + 1, nk, right, carry)
    for g in range(ATTN_ROW_SPLITS):
        (_, a1), (_, a2) = carry[2 * g], carry[2 * g + 1]
        o = (a1[:, :ATTN_V_DIM] / a1[:, ATTN_V_DIM:ATTN_V_DIM + 1]
             - lam * (a2[:, :ATTN_V_DIM] / a2[:, ATTN_V_DIM:ATTN_V_DIM + 1]))
        r = lax.rsqrt(jnp.mean(o * o, axis=-1, keepdims=True) + RMS_EPS)
        o_ref[g * sub:(g + 1) * sub, :] = (o * r * g_ref[...] * (1.0 - lambda_init)).astype(o_ref.dtype)


def _attention(pm, slopes, lamp, norm_g, lambda_init, tile):
    b, s, _ = pm.shape
    kern = functools.partial(_attn_kernel, tile=tile, seq=s, lambda_init=lambda_init)
    return pl.pallas_call(
        kern, grid=(b, ATTN_HEADS, s // tile),
        in_specs=[pl.BlockSpec(memory_space=pltpu.SMEM),
                  pl.BlockSpec((4, ATTN_QK_DIM), lambda bi, h, i: (0, 0)),
                  pl.BlockSpec((1, ATTN_V_DIM), lambda bi, h, i: (0, 0)),
                  pl.BlockSpec((None, tile, LANES), lambda bi, h, i: (bi, i, COL_AQ + h)),
                  pl.BlockSpec((None, s, LANES), lambda bi, h, i: (bi, 0, COL_AK + h)),
                  pl.BlockSpec((None, s, LANES), lambda bi, h, i: (bi, 0, COL_AV + h))],
        out_specs=pl.BlockSpec((None, tile, LANES), lambda bi, h, i: (bi, i, h)),
        out_shape=jax.ShapeDtypeStruct((b, s, ATTN_HEADS * ATTN_V_DIM), BF16),
        scratch_shapes=[pltpu.VMEM((s, 2 * LANES), BF16), pltpu.VMEM((s, 2 * LANES), BF16)],
        compiler_params=_cp(("parallel", "parallel", "arbitrary")), name="diff_attn",
    )(slopes, lamp, norm_g.reshape(1, ATTN_V_DIM), pm, pm, pm)


def _lane_pick(x, idx):
    lane = lax.broadcasted_iota(jnp.int32, x.shape, 1)
    return jnp.sum(jnp.where(lane == idx, x, 0.0), axis=1, keepdims=True)


def _unit_tri_inverses(lms, eye):
    ps = [(-lm).astype(BF16) for lm in lms]
    invs = [eye - lm for lm in lms]
    for _ in range(int(math.log2(CHUNK)) - 1):
        ps = [jnp.dot(p, p, preferred_element_type=F32).astype(BF16) for p in ps]
        invs = [inv + jnp.dot(inv.astype(BF16), p, preferred_element_type=F32)
                for inv, p in zip(invs, ps)]
    return invs


def _gdn_kernel(qr_ref, kr_ref, vr_ref, gt_ref, sm_ref, smt_ref, cw_ref, ng_ref, o_ref,
                q_s, k_s, v_s, qw_s, m_s, n_s, st_s, dec_s, of_s, ob_s, *, seq):
    hh = pl.program_id(1)
    nchunk = seq // CHUNK
    per = GDN_TILE // CHUNK
    hd = DN_HEAD_DIM
    row = lax.broadcasted_iota(jnp.int32, (seq, hd), 0)

    def conv_silu(ref, widx):
        u = ref[...].astype(F32)
        w = cw_ref[widx]
        acc = jnp.zeros_like(u)
        for j in range(CONV_WIDTH):
            d = j - CONV_WIDTH // 2
            if d == 0:
                sh = u
            else:
                sh = pltpu.roll(u, (-d) % seq, axis=0)
                ok = (row + d >= 0) & (row + d < seq)
                sh = jnp.where(ok, sh, 0.0)
            acc = acc + sh * w[j:j + 1, :]
        return _silu(acc)

    def l2n(x):
        return x * lax.rsqrt(jnp.sum(x * x, axis=-1, keepdims=True) + RMS_EPS)

    q_s[...] = l2n(conv_silu(qr_ref, 0)) * (hd ** -0.5)
    k_s[...] = l2n(conv_silu(kr_ref, 1))
    v_s[...] = conv_silu(vr_ref, 2)

    ti = lax.broadcasted_iota(jnp.int32, (GDN_TILE, GDN_TILE), 0)
    tj = lax.broadcasted_iota(jnp.int32, (GDN_TILE, GDN_TILE), 1)
    same = (ti // CHUNK) == (tj // CHUNK)
    incl = (same & (tj <= ti), same & (tj >= ti))
    offdiag = ti != tj
    eye = jnp.where(offdiag, 0.0, 1.0)
    nt = (((1,), (1,)), ((), ()))

    def local(n, _):
        rows = pl.ds(pl.multiple_of(n * GDN_TILE, GDN_TILE), GDN_TILE)
        qc = q_s[rows, :]
        kc = k_s[rows, :]
        vc = v_s[rows, :]
        smc = sm_ref[rows, :]
        k16 = kc.astype(BF16)
        kk = lax.dot_general(k16, k16, nt, preferred_element_type=F32)
        qk = lax.dot_general(qc.astype(BF16), k16, nt, preferred_element_type=F32)
        betas, gcols, decays, lms = [], [], [], []
        for d in range(2):
            beta = _lane_pick(smc, d * DN_HEADS + hh)
            gcol = _lane_pick(smc, 8 + d * DN_HEADS + hh)
            grow = smt_ref[n, pl.ds(8 + d * DN_HEADS + hh, 1), :]
            decay = jnp.where(incl[d], jnp.exp(jnp.where(incl[d], gcol - grow, 0.0)), 0.0)
            betas.append(beta)
            gcols.append(gcol)
            decays.append(decay)
            lms.append(jnp.where(offdiag, beta * kk * decay, 0.0))
        invs = _unit_tri_inverses(lms, eye)
        for d in range(2):
            o_s = of_s if d == 0 else ob_s
            beta, gcol, decay, inv = betas[d], gcols[d], decays[d], invs[d]
            eg = jnp.exp(gcol)
            rhs = jnp.concatenate([vc * beta, kc * (beta * eg)], axis=1)
            sol = jnp.dot(inv.astype(BF16), rhs.astype(BF16), preferred_element_type=F32)
            attn = (qk * decay).astype(BF16)
            auw = jnp.dot(attn, sol.astype(BF16), preferred_element_type=F32)
            o_s[rows, :] = auw[:, :hd]
            qw_s[d, rows, :] = (qc * eg - auw[:, hd:]).astype(BF16)
            last = CHUNK - 1 if d == 0 else 0
            glast = jnp.concatenate(
                [jnp.broadcast_to(gcol[m * CHUNK + last:m * CHUNK + last + 1, :], (CHUNK, 1))
                 for m in range(per)], axis=0)
            kd = kc * jnp.exp(glast - gcol)
            wu16 = jnp.concatenate([sol[:, hd:], sol[:, :hd]], axis=1).astype(BF16)
            for m in range(per):
                c = n * per + m
                blk = slice(m * CHUNK, (m + 1) * CHUNK)
                mats = pl.ds(pl.multiple_of(c * hd, hd), hd)
                mn = jnp.dot(kd[blk].T.astype(BF16), wu16[blk], preferred_element_type=F32)
                m_s[d, mats, :] = (-mn[:, :hd]).astype(BF16)
                n_s[d, mats, :] = mn[:, hd:].astype(BF16)
                dec_s[d, pl.ds(c, 1), :] = jnp.broadcast_to(
                    jnp.exp(gcol[m * CHUNK + last:m * CHUNK + last + 1, :]), (1, hd))
        return 0

    lax.fori_loop(0, seq // GDN_TILE, local, 0, unroll=2)

    def scan(n, states):
        new_states = []
        for d in range(2):
            c = n if d == 0 else nchunk - 1 - n
            mats = pl.ds(pl.multiple_of(c * hd, hd), hd)
            st = states[d]
            s16 = st.astype(BF16)
            st_s[d, mats, :] = s16
            st = (st * dec_s[d, pl.ds(c, 1), :] + n_s[d, mats, :].astype(F32)
                  + jnp.dot(m_s[d, mats, :], s16, preferred_element_type=F32))
            new_states.append(st)
        return tuple(new_states)

    z = jnp.zeros((hd, hd), F32)
    lax.fori_loop(0, nchunk, scan, (z, z))

    def emit(c, _):
        rows = pl.ds(pl.multiple_of(c * CHUNK, CHUNK), CHUNK)
        mats = pl.ds(pl.multiple_of(c * hd, hd), hd)
        for d in range(2):
            o_s = of_s if d == 0 else ob_s
            o_s[rows, :] = o_s[rows, :] + jnp.dot(qw_s[d, rows, :], st_s[d, mats, :],
                                                  preferred_element_type=F32)
        return 0

    lax.fori_loop(0, nchunk, emit, 0, unroll=4)

    o = of_s[...] + ob_s[...]
    r = lax.rsqrt(jnp.mean(o * o, axis=-1, keepdims=True) + RMS_EPS)
    o_ref[...] = (o * r * ng_ref[...] * _silu(gt_ref[...].astype(F32))).astype(o_ref.dtype)


def _gdn(pm, sm, smt, cw, norm_g):
    b, s, _ = pm.shape
    nchunk = s // CHUNK
    ntile = s // GDN_TILE
    col = lambda c0: pl.BlockSpec((None, s, LANES), lambda bi, h: (bi, 0, c0 + h))
    hd = DN_HEAD_DIM
    return pl.pallas_call(
        functools.partial(_gdn_kernel, seq=s), grid=(b, DN_HEADS),
        in_specs=[col(COL_DQ), col(COL_DK), col(COL_DV), col(COL_DG),
                  pl.BlockSpec((None, s, LANES), lambda bi, h: (bi, 0, 0)),
                  pl.BlockSpec((None, ntile, 16, GDN_TILE), lambda bi, h: (bi, 0, 0, 0)),
                  pl.BlockSpec((3, None, CONV_WIDTH, hd), lambda bi, h: (0, h, 0, 0)),
                  pl.BlockSpec((1, hd), lambda bi, h: (0, 0))],
        out_specs=pl.BlockSpec((None, s, LANES), lambda bi, h: (bi, 0, h)),
        out_shape=jax.ShapeDtypeStruct((b, s, DN_HEADS * hd), BF16),
        scratch_shapes=[pltpu.VMEM((s, hd), F32), pltpu.VMEM((s, hd), F32), pltpu.VMEM((s, hd), F32),
                        pltpu.VMEM((2, s, hd), BF16), pltpu.VMEM((2, nchunk * hd, hd), BF16),
                        pltpu.VMEM((2, nchunk * hd, hd), BF16), pltpu.VMEM((2, nchunk * hd, hd), BF16),
                        pltpu.VMEM((2, nchunk, hd), F32),
                        pltpu.VMEM((s, hd), F32), pltpu.VMEM((s, hd), F32)],
        compiler_params=_cp(("parallel", "parallel")), name="gdn",
    )(pm, pm, pm, pm, sm, smt, cw, norm_g.reshape(1, hd))


def _outproj_kernel(h_ref, oa_ref, od_ref, ga_ref, gd_ref, wpa_ref, wpd_ref, wo_ref, g_ref, b_ref,
                    wr_ref, h1_ref, aff_ref):
    pa = jnp.dot(oa_ref[...], wpa_ref[...], preferred_element_type=F32)
    pd = jnp.dot(od_ref[...], wpd_ref[...], preferred_element_type=F32)
    merged = _sigmoid(ga_ref[...].astype(F32)) * pa + _sigmoid(gd_ref[...].astype(F32)) * pd
    mix = jnp.dot(merged.astype(BF16), wo_ref[...], preferred_element_type=F32)
    h1 = _layer_norm(DEEPNORM_ALPHA * h_ref[...] + mix, g_ref[...], b_ref[...])
    h1_ref[...] = h1
    logits = _dot3(h1, wr_ref[...])
    lane = lax.broadcasted_iota(jnp.int32, logits.shape, 1)
    logits = jnp.where(lane < N_EXPERTS, logits, -1e30)
    e = jnp.exp(logits - jnp.max(logits, axis=-1, keepdims=True))
    aff_ref[...] = e / jnp.sum(e, axis=-1, keepdims=True)


def _outproj(h2, oa2, od2, pm2, wpa, wpd, wo, g, b, wr, tm):
    t, d = h2.shape
    row = lambda w, c=0: pl.BlockSpec((tm, w), lambda i: (i, c))
    const = lambda shape: pl.BlockSpec(shape, lambda i: (0, 0))
    return pl.pallas_call(
        _outproj_kernel, grid=(t // tm,),
        in_specs=[row(d), row(512), row(512), row(d, 0), row(d, 1),
                  const((512, d)), const((512, d)), const((d, d)), const((1, d)), const((1, d)),
                  const((d, LANES))],
        out_specs=[row(d), row(LANES)],
        out_shape=[jax.ShapeDtypeStruct((t, d), F32), jax.ShapeDtypeStruct((t, LANES), F32)],
        compiler_params=_cp(("parallel",)), name="outproj",
    )(h2, oa2, od2, pm2, pm2, wpa, wpd, wo, g.reshape(1, d), b.reshape(1, d), wr)


def _topk_kernel(aff_ref, tri_ref, idx_ref, pin_s, *, seq, cap, blk):
    aff = aff_ref[...]
    bits = pltpu.bitcast(aff, jnp.int32)
    ones = lambda cond: jnp.where(cond, 1.0, 0.0)

    def refine(i, thr):
        cand = thr | (jnp.int32(1) << (30 - i))
        cnt = jnp.sum(ones(bits >= cand), axis=0, keepdims=True)
        return jnp.where(cnt >= cap, cand, thr)

    thr = lax.fori_loop(0, 31, refine, jnp.zeros((1, LANES), jnp.int32))
    gt = ones(bits > thr)
    eq = ones(bits == thr)
    need = cap - jnp.sum(gt, axis=0, keepdims=True)

    def prefix(mask):
        carry = jnp.zeros((1, LANES), F32)
        for r in range(0, seq, blk):
            m = mask[r:r + blk, :]
            loc = jnp.dot(tri_ref[...], m.astype(BF16), preferred_element_type=F32)
            pin_s[r:r + blk, :] = loc + carry
            carry = carry + jnp.sum(m, axis=0, keepdims=True)
        return pin_s[...]

    eq_rank = prefix(eq)
    sel = jnp.maximum(gt, eq * ones(eq_rank <= need))
    prefix(sel)
    slot = lax.broadcasted_iota(jnp.int32, (blk, cap), 1).astype(F32)
    for e in range(N_EXPERTS):
        acc = jnp.zeros((1, cap), F32)
        for r in range(0, seq, blk):
            col = pin_s[r:r + blk, e:e + 1]
            acc = acc + jnp.sum(ones(col <= slot), axis=0, keepdims=True)
        idx_ref[e:e + 1, :] = acc.astype(jnp.int32)


def _topk(aff3, tri, cap, blk):
    b, s, _ = aff3.shape
    return pl.pallas_call(
        functools.partial(_topk_kernel, seq=s, cap=cap, blk=blk), grid=(b,),
        in_specs=[pl.BlockSpec((None, s, LANES), lambda bi: (bi, 0, 0)),
                  pl.BlockSpec((blk, blk), lambda bi: (0, 0))],
        out_specs=pl.BlockSpec((None, N_EXPERTS, cap), lambda bi: (bi, 0, 0)),
        out_shape=jax.ShapeDtypeStruct((b, N_EXPERTS, cap), jnp.int32),
        scratch_shapes=[pltpu.VMEM((s, LANES), F32)],
        compiler_params=_cp(("parallel",)), name="topk")(aff3, tri)


def _gather_kernel(idx_ref, x_ref, aff_ref, xg_ref, gg_ref, xs_s, *, cap):
    def body(s, _):
        t = idx_ref[0, s]
        xs_s[pl.ds(s, 1), :] = x_ref[pl.ds(t, 1), :]
        gg_ref[pl.ds(s, 1), :] = aff_ref[pl.ds(t, 1), :]
        return 0

    lax.fori_loop(0, cap, body, 0, unroll=8)
    xg_ref[...] = xs_s[...].astype(xg_ref.dtype)


def _gather(idx3, h3, aff3, cap):
    b, s, d = h3.shape
    return pl.pallas_call(
        functools.partial(_gather_kernel, cap=cap), grid=(b, N_EXPERTS),
        in_specs=[pl.BlockSpec((None, 1, cap), lambda bi, e: (bi * N_EXPERTS + e, 0, 0),
                               memory_space=pltpu.SMEM),
                  pl.BlockSpec((None, s, d), lambda bi, e: (bi, 0, 0)),
                  pl.BlockSpec((None, s, LANES), lambda bi, e: (bi, 0, 0))],
        out_specs=[pl.BlockSpec((None, None, cap, d), lambda bi, e: (bi, e, 0, 0)),
                   pl.BlockSpec((None, None, cap, LANES), lambda bi, e: (bi, e, 0, 0))],
        out_shape=[jax.ShapeDtypeStruct((b, N_EXPERTS, cap, d), BF16),
                   jax.ShapeDtypeStruct((b, N_EXPERTS, cap, LANES), F32)],
        scratch_shapes=[pltpu.VMEM((cap, d), F32)],
        compiler_params=_cp(("parallel", "arbitrary")), name="moe_gather")(idx3, h3, aff3)


def _ffn_kernel(xg_ref, gg_ref, wg_ref, wu_ref, wd_ref, yo_ref):
    e = pl.program_id(0)
    x = xg_ref[...]
    a = jnp.dot(x, wg_ref[...], preferred_element_type=F32)
    u = jnp.dot(x, wu_ref[...], preferred_element_type=F32)
    hid = (_silu(a) * u).astype(BF16)
    y = jnp.dot(hid, wd_ref[...], preferred_element_type=F32)
    yo_ref[...] = y * _lane_pick(gg_ref[...], e)


def _ffn(xg, gg, wg, wu, wd):
    b, ne, cap, d = xg.shape
    f = wg.shape[-1]
    return pl.pallas_call(
        _ffn_kernel, grid=(ne, b),
        in_specs=[pl.BlockSpec((None, None, cap, d), lambda e, bi: (bi, e, 0, 0)),
                  pl.BlockSpec((None, None, cap, LANES), lambda e, bi: (bi, e, 0, 0)),
                  pl.BlockSpec((None, d, f), lambda e, bi: (e, 0, 0)),
                  pl.BlockSpec((None, d, f), lambda e, bi: (e, 0, 0)),
                  pl.BlockSpec((None, f, d), lambda e, bi: (e, 0, 0))],
        out_specs=pl.BlockSpec((None, None, cap, d), lambda e, bi: (bi, e, 0, 0)),
        out_shape=jax.ShapeDtypeStruct((b, ne, cap, d), F32),
        compiler_params=_cp(("parallel", "parallel")), name="moe_ffn")(xg, gg, wg, wu, wd)


def _combine_kernel(idx_ref, yo_ref, acc_ref, *, cap):
    @pl.when(pl.program_id(1) == 0)
    def _():
        acc_ref[...] = jnp.zeros_like(acc_ref)

    def body(s, _):
        t = idx_ref[0, s]
        acc_ref[pl.ds(t, 1), :] = acc_ref[pl.ds(t, 1), :] + yo_ref[pl.ds(s, 1), :]
        return 0

    lax.fori_loop(0, cap, body, 0, unroll=8)


def _combine(idx3, yo, s):
    b, ne, cap, d = yo.shape
    return pl.pallas_call(
        functools.partial(_combine_kernel, cap=cap), grid=(b, ne),
        in_specs=[pl.BlockSpec((None, 1, cap), lambda bi, e: (bi * N_EXPERTS + e, 0, 0),
                               memory_space=pltpu.SMEM),
                  pl.BlockSpec((None, None, cap, d), lambda bi, e: (bi, e, 0, 0))],
        out_specs=pl.BlockSpec((None, s, d), lambda bi, e: (bi, 0, 0)),
        out_shape=jax.ShapeDtypeStruct((b, s, d), F32),
        compiler_params=_cp(("parallel", "arbitrary")), name="moe_combine")(idx3, yo)


def _block_tri(n, lower):
    i = jnp.arange(n)[:, None]
    j = jnp.arange(n)[None, :]
    same = (i // CHUNK) == (j // CHUNK)
    return (same & ((j <= i) if lower else (j >= i))).astype(BF16)


def kernel(x, ln0_g, ln0_b, w_in, lam_q1, lam_k1, lam_q2, lam_k2, attn_norm_g, conv_w, a_log, dt_bias,
           dn_norm_g, w_pa, w_pd, w_o, ln1_g, ln1_b, w_router, w_gate, w_up, w_down, ln2_g, ln2_b):
    b, s, d = x.shape
    t = b * s
    cap = CAPACITY_FACTOR * s // N_EXPERTS
    tm = min(512, s)
    blk = min(512, s)

    slopes = 2.0 ** (-8.0 * jnp.arange(1, ATTN_HEADS + 1, dtype=F32) / ATTN_HEADS)
    mf = _block_tri(tm, True)
    mb = _block_tri(tm, False)
    tri = jnp.tril(jnp.ones((blk, blk), BF16))
    pad16 = lambda v: jnp.pad(v.reshape(1, -1), ((0, 0), (8, LANES - 16)))

    h = _ln(x.reshape(t, d), ln0_g, ln0_b, tm)
    for l in range(DEPTH):
        lambda_init = 0.8 - 0.6 * math.exp(-0.3 * l)
        w = w_in[l]
        wm = jnp.concatenate([w[:, 3600:], w[:, :3584]], axis=1).astype(BF16)
        ws = jnp.pad(w[:, 3584:3600], ((0, 0), (0, LANES - 16)))
        pm, sm, smt = _inproj(h, wm, ws, pad16(a_log[l]), pad16(dt_bias[l]), mf, mb, tm)
        pm3 = pm.reshape(b, s, MAIN_W)
        lamp = jnp.stack([lam_q1[l], lam_k1[l], lam_q2[l], lam_k2[l]])
        oa = _attention(pm3, slopes, lamp, attn_norm_g[l], lambda_init, min(ATTN_TILE, s))
        cw = conv_w[l].reshape(CONV_WIDTH, 3, DN_HEADS, DN_HEAD_DIM).transpose(1, 2, 0, 3)
        od = _gdn(pm3, sm.reshape(b, s, LANES), smt.reshape(b, s // GDN_TILE, 16, GDN_TILE), cw,
                  dn_norm_g[l])
        wr = jnp.pad(w_router[l], ((0, 0), (0, LANES - N_EXPERTS)))
        h1, aff = _outproj(h, oa.reshape(t, -1), od.reshape(t, -1), pm,
                           w_pa[l].astype(BF16), w_pd[l].astype(BF16), w_o[l].astype(BF16),
                           ln1_g[l], ln1_b[l], wr, tm)
        aff3 = aff.reshape(b, s, LANES)
        idx = _topk(aff3, tri, cap, blk)
        idx3 = idx.reshape(b * N_EXPERTS, 1, cap)
        xg, gg = _gather(idx3, h1.reshape(b, s, d), aff3, cap)
        yo = _ffn(xg, gg, w_gate[l].astype(BF16), w_up[l].astype(BF16), w_down[l].astype(BF16))
        ffn = _combine(idx3, yo, s)
        h = _add_ln(h1, ffn.reshape(t, d), ln2_g[l], ln2_b[l], tm)
    return h.reshape(b, s, d)
```
